```python
import math
import jax, jax.numpy as jnp
from jax import lax
import numpy as np

D_MODEL = 1024
BATCH = 8
SEQ = 2048
DEPTH = 1

CHUNK = 64
SSD_DIM = D_MODEL
SSD_HEAD_DIM = 64
SSD_HEADS = SSD_DIM // SSD_HEAD_DIM
SSD_NGROUPS = 2
SSD_STATE = 128
SSD_CONV = 4
POOL_DIM = D_MODEL
POOL_WINDOWS = (2, 4, 8, 16)
POOL_GROUPS = len(POOL_WINDOWS)
POOL_GROUP_DIM = POOL_DIM // POOL_GROUPS
MIX_DIM = SSD_DIM + POOL_DIM
CONV_DIM = SSD_DIM + 2 * SSD_NGROUPS * SSD_STATE
IN_DIM = SSD_DIM + CONV_DIM + SSD_HEADS + POOL_DIM
D_FF = 2816
NORM_EPS = 1e-6

kernel_name = "hybrid_ssd_pool_macaron_block"


def rms_norm(x, g):
    xf = x.astype(jnp.float32)
    y = xf * lax.rsqrt(jnp.mean(xf * xf, axis=-1, keepdims=True) + NORM_EPS)
    return (y * g.astype(jnp.float32)).astype(x.dtype)


def swiglu(h, w_gate, w_up, w_down):
    return (jax.nn.silu(h @ w_gate) * (h @ w_up)) @ w_down


def causal_depthwise_conv(u, w, b):
    c = u.shape[-1]
    y = lax.conv_general_dilated(
        u, w.astype(u.dtype)[:, None, :], window_strides=(1,),
        padding=[(SSD_CONV - 1, 0)], dimension_numbers=('NWC', 'WIO', 'NWC'),
        feature_group_count=c)
    return y + b.astype(u.dtype)


def ssd_chunked_scan(xh, dt, A, Bh, Ch):
    b, s, h, p = xh.shape
    n = Bh.shape[-1]
    c = s // CHUNK
    X = (xh * dt[..., None]).reshape(b, c, CHUNK, h, p)
    Bc = Bh.reshape(b, c, CHUNK, h, n)
    Cc = Ch.reshape(b, c, CHUNK, h, n)
    A_cs = jnp.cumsum((dt * A).reshape(b, c, CHUNK, h).transpose(0, 3, 1, 2), axis=-1)
    causal = jnp.tril(jnp.ones((CHUNK, CHUNK), dtype=bool))
    L = jnp.exp(jnp.where(causal, A_cs[..., :, None] - A_cs[..., None, :], -jnp.inf))
    y_diag = jnp.einsum('bclhn,bcshn,bhcls,bcshp->bclhp', Cc, Bc, L, X)
    decay_states = jnp.exp(A_cs[..., -1:] - A_cs)
    states = jnp.einsum('bclhn,bhcl,bclhp->bchpn', Bc, decay_states, X)
    chunk_decay = jnp.exp(A_cs[..., -1])

    def step(carry, inp):
        st, dec = inp
        return dec[..., None, None] * carry + st, carry

    _, prev = lax.scan(step, jnp.zeros((b, h, p, n), jnp.float32),
                       (states.transpose(1, 0, 2, 3, 4), chunk_decay.transpose(2, 0, 1)))
    prev = prev.transpose(1, 0, 2, 3, 4)
    y_off = jnp.einsum('bclhn,bchpn,bhcl->bclhp', Cc, prev, jnp.exp(A_cs))
    return (y_diag + y_off).reshape(b, s, h, p)


def ssd_group(z, xbc, dt_raw, conv_w, conv_b, dt_bias, a_log, d_skip, ssd_norm):
    b, s, _ = z.shape
    xbc = jax.nn.silu(causal_depthwise_conv(xbc, conv_w, conv_b)).astype(jnp.float32)
    gn = SSD_NGROUPS * SSD_STATE
    xs = xbc[..., :SSD_DIM].reshape(b, s, SSD_HEADS, SSD_HEAD_DIM)
    rep = SSD_HEADS // SSD_NGROUPS
    Bh = jnp.repeat(xbc[..., SSD_DIM:SSD_DIM + gn].reshape(b, s, SSD_NGROUPS, SSD_STATE), rep, axis=2)
    Ch = jnp.repeat(xbc[..., SSD_DIM + gn:].reshape(b, s, SSD_NGROUPS, SSD_STATE), rep, axis=2)
    dt = jax.nn.softplus(dt_raw.astype(jnp.float32) + dt_bias.astype(jnp.float32))
    A = -jnp.exp(a_log.astype(jnp.float32))
    y = ssd_chunked_scan(xs, dt, A, Bh, Ch) + d_skip.astype(jnp.float32)[:, None] * xs
    yg = (y.reshape(b, s, SSD_DIM) * jax.nn.silu(z.astype(jnp.float32))).reshape(b, s, SSD_NGROUPS, -1)
    yg = yg * lax.rsqrt(jnp.mean(yg * yg, axis=-1, keepdims=True) + NORM_EPS)
    return (yg.reshape(b, s, SSD_DIM) * ssd_norm.astype(jnp.float32)).astype(z.dtype)


def multiscale_pool_group(u, pool_w, pool_scale):
    b, s, _ = u.shape
    uf = u.astype(jnp.float32).reshape(b, s, POOL_GROUPS, POOL_GROUP_DIM)
    csp = jnp.pad(jnp.cumsum(uf, axis=1), ((0, 0), (1, 0), (0, 0), (0, 0)))
    t = jnp.arange(s)
    outs = []
    for gi, win in enumerate(POOL_WINDOWS):
        cg = csp[:, :, gi]
        lo = jnp.pad(cg[:, :s + 1 - win], ((0, 0), (win - 1, 0), (0, 0)))
        cnt = jnp.minimum(t + 1, win).astype(jnp.float32)[None, :, None]
        outs.append((cg[:, 1:] - lo) / cnt - uf[:, :, gi])
    pooled = jnp.stack(outs, axis=2)
    y = jnp.einsum('bsgc,gcd->bsgd', pooled, pool_w.astype(jnp.float32)).reshape(b, s, POOL_DIM)
    return (y * pool_scale.astype(jnp.float32)).astype(u.dtype)


def setup_inputs(seed: int = 0) -> dict:
    key = jax.random.key(seed)
    ks = jax.random.split(key, 24)
    f32 = jnp.float32

    def nrm(k, shape, scale):
        return jax.random.normal(k, shape, f32) * scale

    def gain(k, shape):
        return 1.0 + 0.05 * jax.random.normal(k, shape, f32)

    L = DEPTH
    dt0 = jnp.exp(jax.random.uniform(ks[9], (L, SSD_HEADS), f32) * (math.log(0.1) - math.log(1e-3)) + math.log(1e-3))
    dt_bias = dt0 + jnp.log(-jnp.expm1(-dt0))
    return {
        "x": jax.random.normal(ks[0], (BATCH, SEQ, D_MODEL), f32),
        "ffn1_norm": gain(ks[1], (L, D_MODEL)),
        "ffn1_w_gate": nrm(ks[2], (L, D_MODEL, D_FF), D_MODEL ** -0.5),
        "ffn1_w_up": nrm(ks[3], (L, D_MODEL, D_FF), D_MODEL ** -0.5),
        "ffn1_w_down": nrm(ks[4], (L, D_FF, D_MODEL), D_FF ** -0.5),
        "mix_norm": gain(ks[5], (L, D_MODEL)),
        "w_in": nrm(ks[6], (L, D_MODEL, IN_DIM), D_MODEL ** -0.5),
        "conv_w": nrm(ks[7], (L, SSD_CONV, CONV_DIM), SSD_CONV ** -0.5),
        "conv_b": nrm(ks[8], (L, CONV_DIM), 0.02),
        "dt_bias": dt_bias,
        "a_log": jnp.log(jax.random.uniform(ks[10], (L, SSD_HEADS), f32, 1.0, 16.0)),
        "d_skip": gain(ks[11], (L, SSD_HEADS)),
        "ssd_norm": gain(ks[12], (L, SSD_DIM)),
        "pool_w": nrm(ks[13], (L, POOL_GROUPS, POOL_GROUP_DIM, POOL_GROUP_DIM), POOL_GROUP_DIM ** -0.5),
        "pool_scale": gain(ks[14], (L, POOL_DIM)),
        "w_out": nrm(ks[15], (L, MIX_DIM, D_MODEL), MIX_DIM ** -0.5),
        "ffn2_norm": gain(ks[16], (L, D_MODEL)),
        "ffn2_w_gate": nrm(ks[17], (L, D_MODEL, D_FF), D_MODEL ** -0.5),
        "ffn2_w_up": nrm(ks[18], (L, D_MODEL, D_FF), D_MODEL ** -0.5),
        "ffn2_w_down": nrm(ks[19], (L, D_FF, D_MODEL), D_FF ** -0.5),
        "final_norm": gain(ks[20], (D_MODEL,)),
    }


def reference(x, ffn1_norm, ffn1_w_gate, ffn1_w_up, ffn1_w_down, mix_norm, w_in, conv_w, conv_b,
              dt_bias, a_log, d_skip, ssd_norm, pool_w, pool_scale, w_out, ffn2_norm,
              ffn2_w_gate, ffn2_w_up, ffn2_w_down, final_norm):
    for i in range(DEPTH):
        x = x + 0.5 * swiglu(rms_norm(x, ffn1_norm[i]), ffn1_w_gate[i], ffn1_w_up[i], ffn1_w_down[i])
        h = rms_norm(x, mix_norm[i])
        proj = h @ w_in[i]
        o1 = SSD_DIM
        o2 = o1 + CONV_DIM
        o3 = o2 + SSD_HEADS
        z, xbc, dt_raw, u = proj[..., :o1], proj[..., o1:o2], proj[..., o2:o3], proj[..., o3:]
        y_ssd = ssd_group(z, xbc, dt_raw, conv_w[i], conv_b[i], dt_bias[i], a_log[i], d_skip[i], ssd_norm[i])
        y_pool = multiscale_pool_group(u, pool_w[i], pool_scale[i])
        x = x + jnp.concatenate([y_ssd, y_pool], axis=-1) @ w_out[i]
        x = x + 0.5 * swiglu(rms_norm(x, ffn2_norm[i]), ffn2_w_gate[i], ffn2_w_up[i], ffn2_w_down[i])
    return rms_norm(x, final_norm)
```

```python
import functools

import numpy as np
import jax
import jax.numpy as jnp
from jax import lax
from jax.experimental import pallas as pl
from jax.experimental.pallas import tpu as pltpu

D_MODEL = 1024
CHUNK = 64
SSD_DIM = 1024
SSD_HEAD_DIM = 64
SSD_HEADS = 16
SSD_NGROUPS = 2
SSD_STATE = 128
SSD_CONV = 4
POOL_DIM = 1024
POOL_WINDOWS = (2, 4, 8, 16)
POOL_GROUP_DIM = 256
CONV_DIM = SSD_DIM + 2 * SSD_NGROUPS * SSD_STATE
D_FF = 2816
NORM_EPS = 1e-6

GROUP_DIM = SSD_DIM // SSD_NGROUPS
BC_DIM = SSD_NGROUPS * SSD_STATE
LANES = 128
SUBLANES = 8
HEAD_SLOT = 16
CONV_HIST = SUBLANES
POOL_HIST = 16

FFN_TM = 512
FFN_TF = 256
MIX_TS = 256
VMEM_LIMIT = 56 * 1024 * 1024

F32 = jnp.float32
BF16 = jnp.bfloat16


def _silu(v):
    return v * jax.nn.sigmoid(v)


def _rms(v, gain):
    return v * lax.rsqrt(jnp.mean(v * v, axis=-1, keepdims=True) + NORM_EPS) * gain


def _dot(a, b):
    return jnp.dot(a, b, preferred_element_type=F32)


def _ffn_kernel(x_ref, g_ref, wg_ref, wu_ref, wd_ref, fg_ref, o_ref, act_ref, *, final_norm):
    x = x_ref[...]
    h = _rms(x, g_ref[...]).astype(BF16)
    for j in range(D_FF // FFN_TF):
        cols = slice(j * FFN_TF, (j + 1) * FFN_TF)
        gate = _dot(h, wg_ref[:, cols])
        up = _dot(h, wu_ref[:, cols])
        act_ref[:, cols] = (_silu(gate) * up).astype(BF16)
    y = x + 0.5 * _dot(act_ref[...], wd_ref[...])
    if final_norm:
        y = _rms(y, fg_ref[...])
    o_ref[...] = y


def _resident(shape):
    return pl.BlockSpec(shape, lambda *_: (0,) * len(shape), pipeline_mode=pl.Buffered(1))


def _ffn(x2d, gain, w_gate, w_up, w_down, final_gain, *, final_norm, name):
    m = x2d.shape[0]
    return pl.pallas_call(
        functools.partial(_ffn_kernel, final_norm=final_norm),
        out_shape=jax.ShapeDtypeStruct((m, D_MODEL), F32),
        grid=(m // FFN_TM,),
        in_specs=[
            pl.BlockSpec((FFN_TM, D_MODEL), lambda i: (i, 0)),
            _resident((1, D_MODEL)),
            _resident((D_MODEL, D_FF)),
            _resident((D_MODEL, D_FF)),
            _resident((D_FF, D_MODEL)),
            _resident((1, D_MODEL)),
        ],
        out_specs=pl.BlockSpec((FFN_TM, D_MODEL), lambda i: (i, 0)),
        scratch_shapes=[pltpu.VMEM((FFN_TM, D_FF), BF16)],
        compiler_params=pltpu.CompilerParams(
            dimension_semantics=("arbitrary",), vmem_limit_bytes=VMEM_LIMIT),
        name=name,
    )(x2d, gain, w_gate, w_up, w_down, final_gain)


def _split3(v):
    hi = v.astype(BF16).astype(F32)
    r = v - hi
    mid = r.astype(BF16).astype(F32)
    return hi, mid, r - mid


def _mix_kernel(x_ref, g_ref, wz_ref, wxbc_ref, wdt_ref, wu_ref, convw_ref, convb_ref,
                dtb_ref, alog_ref, dskip_ref, ssdn_ref, expand_ref, poolw_ref, pools_ref,
                wo_ref, o_ref,
                xext, uext, state, z_s, xs_s, bc_s, ee_s, y_s):
    ts = MIX_TS
    sj = pl.program_id(1)

    @pl.when(sj == 0)
    def _():
        xext[0:CONV_HIST, :] = jnp.zeros((CONV_HIST, CONV_DIM), F32)
        uext[0:POOL_HIST, :] = jnp.zeros((POOL_HIST, POOL_DIM), F32)
        state[...] = jnp.zeros_like(state)

    x = x_ref[0]
    h = _rms(x, g_ref[...]).astype(BF16)
    z_s[...] = _dot(h, wz_ref[...])
    xext[CONV_HIST:CONV_HIST + ts, :] = _dot(h, wxbc_ref[...])
    uext[POOL_HIST:POOL_HIST + ts, :] = _dot(h, wu_ref[...])

    dt_raw = _dot(h, wdt_ref[...]) + dtb_ref[...]
    dt = jnp.maximum(dt_raw, 0.0) + jnp.log1p(jnp.exp(-jnp.abs(dt_raw)))
    cs = dt * (-jnp.exp(alog_ref[...]))
    row_in_chunk = lax.broadcasted_iota(jnp.int32, (ts, LANES), 0) % CHUNK
    shift = 1
    while shift < CHUNK:
        cs = cs + jnp.where(row_in_chunk >= shift, pltpu.roll(cs, shift, axis=0), 0.0)
        shift *= 2

    slot = lax.broadcasted_iota(jnp.int32, (ts, LANES), 1) // HEAD_SLOT
    terms = _split3(dt) + _split3(cs)
    packed = jnp.zeros((ts, LANES), F32)
    for k, term in enumerate(terms):
        packed = jnp.where(slot == k, term, packed)
    ee_s[...] = _dot(packed.astype(BF16), expand_ref[...])

    conv = convb_ref[...] + convw_ref[SSD_CONV - 1:SSD_CONV, :] * xext[CONV_HIST:CONV_HIST + ts, :]
    for k in range(SSD_CONV - 1):
        off = CONV_HIST - (SSD_CONV - 1) + k
        conv = conv + convw_ref[k:k + 1, :] * xext[off:off + ts, :]
    conv = _silu(conv)
    xs_s[...] = conv[:, :SSD_DIM]
    bc_s[...] = conv[:, SSD_DIM:]
    xext[0:CONV_HIST, :] = xext[ts:ts + CONV_HIST, :]

    sub = lax.broadcasted_iota(jnp.int32, (CHUNK, LANES), 0)
    lane = lax.broadcasted_iota(jnp.int32, (CHUNK, LANES), 1)
    diag = sub == lane % CHUNK
    causal = sub >= lane % CHUNK
    left = lane < CHUNK
    for c in range(ts // CHUNK):
        rows = slice(c * CHUNK, (c + 1) * CHUNK)
        for g in range(SSD_NGROUPS):
            gcols = slice(g * GROUP_DIM, (g + 1) * GROUP_DIM)
            xg = xs_s[rows, gcols]
            b_mat = bc_s[rows, g * SSD_STATE:(g + 1) * SSD_STATE]
            c_mat = bc_s[rows, BC_DIM + g * SSD_STATE:BC_DIM + (g + 1) * SSD_STATE]
            dt_e = ee_s[rows, gcols]
            cs_e = ee_s[rows, SSD_DIM + g * GROUP_DIM:SSD_DIM + (g + 1) * GROUP_DIM]
            cs_last = cs_e[CHUNK - 1:CHUNK, :]
            xdt = xg * dt_e
            xdt_b = xdt.astype(BF16)
            x_decayed = (xdt * jnp.exp(cs_last - cs_e)).astype(BF16)
            c_b = c_mat.astype(BF16)
            b_b = b_mat.astype(BF16)
            s_prev = state[g]
            y_off = _dot(c_b, s_prev.astype(BF16)) * jnp.exp(cs_e)
            s_new = _dot(b_mat.T.astype(BF16), x_decayed)
            state[g] = jnp.exp(cs_last) * s_prev + s_new
            cb2 = lax.dot_general(c_b, jnp.concatenate([b_b, b_b], axis=0),
                                  (((1,), (1,)), ((), ())), preferred_element_type=F32)
            for p in range(GROUP_DIM // LANES):
                pc = slice(p * LANES, (p + 1) * LANES)
                col = cs_e[:, pc]
                rowv = jnp.sum(jnp.where(diag, col, 0.0), axis=0, keepdims=True)
                decay = jnp.exp(jnp.where(causal, col - rowv, -jnp.inf))
                g_pair = (cb2 * decay).astype(BF16)
                xp = xdt_b[:, pc]
                zero = jnp.zeros_like(xp)
                block_diag = jnp.concatenate(
                    [jnp.where(left, xp, zero), jnp.where(left, zero, xp)], axis=0)
                oc = slice(g * GROUP_DIM + p * LANES, g * GROUP_DIM + (p + 1) * LANES)
                y_s[rows, oc] = (_dot(g_pair, block_diag) + y_off[:, pc]
                                 + dskip_ref[:, oc] * xg[:, pc])

    yg = y_s[...] * _silu(z_s[...])
    normed = []
    for g in range(SSD_NGROUPS):
        seg = yg[:, g * GROUP_DIM:(g + 1) * GROUP_DIM]
        normed.append(seg * lax.rsqrt(jnp.mean(seg * seg, axis=-1, keepdims=True) + NORM_EPS))
    y_ssd = (jnp.concatenate(normed, axis=-1) * ssdn_ref[...]).astype(BF16)

    t_glob = sj * ts + lax.broadcasted_iota(jnp.int32, (ts, 1), 0)
    pooled = []
    for gi, win in enumerate(POOL_WINDOWS):
        pcols = slice(gi * POOL_GROUP_DIM, (gi + 1) * POOL_GROUP_DIM)
        u0 = uext[POOL_HIST:POOL_HIST + ts, pcols]
        acc = u0
        for j in range(1, win):
            acc = acc + uext[POOL_HIST - j:POOL_HIST - j + ts, pcols]
        inv_cnt = 1.0 / jnp.minimum(t_glob + 1, win).astype(F32)
        pooled.append(_dot((acc * inv_cnt - u0).astype(BF16), poolw_ref[gi]))
    y_pool = (jnp.concatenate(pooled, axis=-1) * pools_ref[...]).astype(BF16)
    uext[0:POOL_HIST, :] = uext[ts:ts + POOL_HIST, :]

    o_ref[0] = x + _dot(y_ssd, wo_ref[0:SSD_DIM, :]) + _dot(y_pool, wo_ref[SSD_DIM:, :])


def _expand_matrix():
    e = np.zeros((LANES, 2 * SSD_DIM), np.float32)
    for k in range(6):
        for hd in range(SSD_HEADS):
            base = (k // 3) * SSD_DIM + hd * SSD_HEAD_DIM
            e[k * HEAD_SLOT + hd, base:base + SSD_HEAD_DIM] = 1.0
    return e


def _mix(x, gain, wz, wxbc, wdt, wu, conv_w, conv_b, dt_bias, a_log, d_skip_e, ssd_norm,
         pool_w, pool_scale, w_out):
    b, s, _ = x.shape
    ts = MIX_TS
    expand = jnp.asarray(_expand_matrix(), BF16)
    operands = (x, gain, wz, wxbc, wdt, wu, conv_w, conv_b, dt_bias, a_log, d_skip_e, ssd_norm,
                expand, pool_w, pool_scale, w_out)
    in_specs = [pl.BlockSpec((1, ts, D_MODEL), lambda i, j: (i, j, 0))]
    in_specs += [_resident(op.shape) for op in operands[1:]]
    return pl.pallas_call(
        _mix_kernel,
        out_shape=jax.ShapeDtypeStruct((b, s, D_MODEL), F32),
        grid=(b, s // ts),
        in_specs=in_specs,
        out_specs=pl.BlockSpec((1, ts, D_MODEL), lambda i, j: (i, j, 0)),
        scratch_shapes=[
            pltpu.VMEM((CONV_HIST + ts, CONV_DIM), F32),
            pltpu.VMEM((POOL_HIST + ts, POOL_DIM), F32),
            pltpu.VMEM((SSD_NGROUPS, SSD_STATE, GROUP_DIM), F32),
            pltpu.VMEM((ts, SSD_DIM), F32),
            pltpu.VMEM((ts, SSD_DIM), F32),
            pltpu.VMEM((ts, 2 * BC_DIM), F32),
            pltpu.VMEM((ts, 2 * SSD_DIM), F32),
            pltpu.VMEM((ts, SSD_DIM), F32),
        ],
        compiler_params=pltpu.CompilerParams(
            dimension_semantics=("arbitrary", "arbitrary"), vmem_limit_bytes=VMEM_LIMIT),
        name="mix",
    )(*operands)


def kernel(x, ffn1_norm, ffn1_w_gate, ffn1_w_up, ffn1_w_down, mix_norm, w_in, conv_w, conv_b,
           dt_bias, a_log, d_skip, ssd_norm, pool_w, pool_scale, w_out, ffn2_norm,
           ffn2_w_gate, ffn2_w_up, ffn2_w_down, final_norm):
    b, s, d = x.shape
    depth = ffn1_norm.shape[0]
    o1 = SSD_DIM
    o2 = o1 + CONV_DIM
    o3 = o2 + SSD_HEADS
    rep = LANES // SSD_HEADS
    row = lambda v: v.reshape(1, -1)
    ones = jnp.ones((1, d), F32)
    for i in range(depth):
        last = i == depth - 1
        x = _ffn(x.reshape(b * s, d), row(ffn1_norm[i]), ffn1_w_gate[i].astype(BF16),
                 ffn1_w_up[i].astype(BF16), ffn1_w_down[i].astype(BF16), ones,
                 final_norm=False, name="ffn1").reshape(b, s, d)
        wi = w_in[i]
        x = _mix(
            x, row(mix_norm[i]),
            wi[:, :o1].astype(BF16), wi[:, o1:o2].astype(BF16),
            jnp.tile(wi[:, o2:o3], (1, rep)).astype(BF16), wi[:, o3:].astype(BF16),
            conv_w[i], row(conv_b[i]),
            jnp.tile(row(dt_bias[i]), (1, rep)), jnp.tile(row(a_log[i]), (1, rep)),
            row(jnp.repeat(d_skip[i], SSD_HEAD_DIM)), row(ssd_norm[i]),
            pool_w[i].astype(BF16), row(pool_scale[i]), w_out[i].astype(BF16))
        x = _ffn(x.reshape(b * s, d), row(ffn2_norm[i]), ffn2_w_gate[i].astype(BF16),
                 ffn2_w_up[i].astype(BF16), ffn2_w_down[i].astype(BF16),
                 row(final_norm) if last else ones,
                 final_norm=last, name="ffn2").reshape(b, s, d)
    if depth == 0:
        x = x * lax.rsqrt(jnp.mean(x * x, axis=-1, keepdims=True) + NORM_EPS) * final_norm
    return x
```

```python
import functools

import numpy as np
import jax
import jax.numpy as jnp
from jax import lax
from jax.experimental import pallas as pl
from jax.experimental.pallas import tpu as pltpu

D_MODEL = 1024
CHUNK = 64
SSD_DIM = 1024
SSD_HEAD_DIM = 64
SSD_HEADS = 16
SSD_NGROUPS = 2
SSD_STATE = 128
SSD_CONV = 4
POOL_DIM = 1024
POOL_WINDOWS = (2, 4, 8, 16)
POOL_GROUP_DIM = 256
CONV_DIM = SSD_DIM + 2 * SSD_NGROUPS * SSD_STATE
D_FF = 2816
NORM_EPS = 1e-6

GROUP_DIM = SSD_DIM // SSD_NGROUPS
BC_DIM = SSD_NGROUPS * SSD_STATE
LANES = 128
SUBLANES = 8
HEAD_SLOT = 16
CONV_HIST = SUBLANES
POOL_HIST = 16

FFN_TM = 512
FFN_TF = 256
MIX_TS = 512
VMEM_LIMIT = 56 * 1024 * 1024

F32 = jnp.float32
BF16 = jnp.bfloat16


def _silu(v):
    return v * jax.nn.sigmoid(v)


def _rms(v, gain):
    return v * lax.rsqrt(jnp.mean(v * v, axis=-1, keepdims=True) + NORM_EPS) * gain


def _dot(a, b):
    return jnp.dot(a, b, preferred_element_type=F32)


def _ffn_kernel(x_ref, g_ref, wg_ref, wu_ref, wd_ref, fg_ref, o_ref, act_ref, *, final_norm):
    x = x_ref[...]
    h = _rms(x, g_ref[...]).astype(BF16)
    for j in range(D_FF // FFN_TF):
        cols = slice(j * FFN_TF, (j + 1) * FFN_TF)
        gate = _dot(h, wg_ref[:, cols])
        up = _dot(h, wu_ref[:, cols])
        act_ref[:, cols] = (_silu(gate) * up).astype(BF16)
    y = x + 0.5 * _dot(act_ref[...], wd_ref[...])
    if final_norm:
        y = _rms(y, fg_ref[...])
    o_ref[...] = y


def _resident(shape):
    return pl.BlockSpec(shape, lambda *_: (0,) * len(shape), pipeline_mode=pl.Buffered(1))


def _ffn(x2d, gain, w_gate, w_up, w_down, final_gain, *, final_norm, name):
    m = x2d.shape[0]
    return pl.pallas_call(
        functools.partial(_ffn_kernel, final_norm=final_norm),
        out_shape=jax.ShapeDtypeStruct((m, D_MODEL), F32),
        grid=(m // FFN_TM,),
        in_specs=[
            pl.BlockSpec((FFN_TM, D_MODEL), lambda i: (i, 0)),
            _resident((1, D_MODEL)),
            _resident((D_MODEL, D_FF)),
            _resident((D_MODEL, D_FF)),
            _resident((D_FF, D_MODEL)),
            _resident((1, D_MODEL)),
        ],
        out_specs=pl.BlockSpec((FFN_TM, D_MODEL), lambda i: (i, 0)),
        scratch_shapes=[pltpu.VMEM((FFN_TM, D_FF), BF16)],
        compiler_params=pltpu.CompilerParams(
            dimension_semantics=("arbitrary",), vmem_limit_bytes=VMEM_LIMIT),
        name=name,
    )(x2d, gain, w_gate, w_up, w_down, final_gain)


def _split3(v):
    hi = v.astype(BF16).astype(F32)
    r = v - hi
    mid = r.astype(BF16).astype(F32)
    return hi, mid, r - mid


def _mix_kernel(x_ref, g_ref, wz_ref, wxbc_ref, wdt_ref, wu_ref, convw_ref, convb_ref,
                dtb_ref, alog_ref, dskip_ref, ssdn_ref, expand_ref, poolw_ref, pools_ref,
                wo_ref, o_ref,
                xext, uext, state, z_s, xs_s, bc_s, ee_s, y_s):
    ts = MIX_TS
    sj = pl.program_id(1)

    @pl.when(sj == 0)
    def _():
        xext[0:CONV_HIST, :] = jnp.zeros((CONV_HIST, CONV_DIM), F32)
        uext[0:POOL_HIST, :] = jnp.zeros((POOL_HIST, POOL_DIM), F32)
        state[...] = jnp.zeros_like(state)

    x = x_ref[0]
    h = _rms(x, g_ref[...]).astype(BF16)
    z_s[...] = _dot(h, wz_ref[...])
    xext[CONV_HIST:CONV_HIST + ts, :] = _dot(h, wxbc_ref[...])
    uext[POOL_HIST:POOL_HIST + ts, :] = _dot(h, wu_ref[...])

    dt_raw = _dot(h, wdt_ref[...]) + dtb_ref[...]
    dt = jnp.maximum(dt_raw, 0.0) + jnp.log1p(jnp.exp(-jnp.abs(dt_raw)))
    cs = dt * (-jnp.exp(alog_ref[...]))
    row_in_chunk = lax.broadcasted_iota(jnp.int32, (ts, LANES), 0) % CHUNK
    shift = 1
    while shift < CHUNK:
        cs = cs + jnp.where(row_in_chunk >= shift, pltpu.roll(cs, shift, axis=0), 0.0)
        shift *= 2

    slot = lax.broadcasted_iota(jnp.int32, (ts, LANES), 1) // HEAD_SLOT
    terms = _split3(dt) + _split3(cs)
    packed = jnp.zeros((ts, LANES), F32)
    for k, term in enumerate(terms):
        packed = jnp.where(slot == k, term, packed)
    ee_s[...] = _dot(packed.astype(BF16), expand_ref[...])

    conv = convb_ref[...] + convw_ref[SSD_CONV - 1:SSD_CONV, :] * xext[CONV_HIST:CONV_HIST + ts, :]
    for k in range(SSD_CONV - 1):
        off = CONV_HIST - (SSD_CONV - 1) + k
        conv = conv + convw_ref[k:k + 1, :] * xext[off:off + ts, :]
    conv = _silu(conv)
    xs_s[...] = conv[:, :SSD_DIM]
    bc_s[...] = conv[:, SSD_DIM:]
    xext[0:CONV_HIST, :] = xext[ts:ts + CONV_HIST, :]

    sub = lax.broadcasted_iota(jnp.int32, (CHUNK, LANES), 0)
    lane = lax.broadcasted_iota(jnp.int32, (CHUNK, LANES), 1)
    diag = sub == lane % CHUNK
    causal = sub >= lane % CHUNK
    left = lane < CHUNK
    for c in range(ts // CHUNK):
        rows = slice(c * CHUNK, (c + 1) * CHUNK)
        for g in range(SSD_NGROUPS):
            gcols = slice(g * GROUP_DIM, (g + 1) * GROUP_DIM)
            xg = xs_s[rows, gcols]
            b_mat = bc_s[rows, g * SSD_STATE:(g + 1) * SSD_STATE]
            c_mat = bc_s[rows, BC_DIM + g * SSD_STATE:BC_DIM + (g + 1) * SSD_STATE]
            dt_e = ee_s[rows, gcols]
            cs_e = ee_s[rows, SSD_DIM + g * GROUP_DIM:SSD_DIM + (g + 1) * GROUP_DIM]
            cs_last = cs_e[CHUNK - 1:CHUNK, :]
            xdt = xg * dt_e
            xdt_b = xdt.astype(BF16)
            x_decayed = (xdt * jnp.exp(cs_last - cs_e)).astype(BF16)
            c_b = c_mat.astype(BF16)
            b_b = b_mat.astype(BF16)
            s_prev = state[g]
            y_off = _dot(c_b, s_prev.astype(BF16)) * jnp.exp(cs_e)
            s_new = _dot(b_mat.T.astype(BF16), x_decayed)
            state[g] = jnp.exp(cs_last) * s_prev + s_new
            cb2 = lax.dot_general(c_b, jnp.concatenate([b_b, b_b], axis=0),
                                  (((1,), (1,)), ((), ())), preferred_element_type=F32)
            for p in range(GROUP_DIM // LANES):
                pc = slice(p * LANES, (p + 1) * LANES)
                col = cs_e[:, pc]
                rowv = jnp.sum(jnp.where(diag, col, 0.0), axis=0, keepdims=True)
                decay = jnp.exp(jnp.where(causal, col - rowv, -jnp.inf))
                g_pair = (cb2 * decay).astype(BF16)
                xp = xdt_b[:, pc]
                zero = jnp.zeros_like(xp)
                block_diag = jnp.concatenate(
                    [jnp.where(left, xp, zero), jnp.where(left, zero, xp)], axis=0)
                oc = slice(g * GROUP_DIM + p * LANES, g * GROUP_DIM + (p + 1) * LANES)
                y_s[rows, oc] = (_dot(g_pair, block_diag) + y_off[:, pc]
                                 + dskip_ref[:, oc] * xg[:, pc])

    yg = y_s[...] * _silu(z_s[...])
    normed = []
    for g in range(SSD_NGROUPS):
        seg = yg[:, g * GROUP_DIM:(g + 1) * GROUP_DIM]
        normed.append(seg * lax.rsqrt(jnp.mean(seg * seg, axis=-1, keepdims=True) + NORM_EPS))
    y_ssd = (jnp.concatenate(normed, axis=-1) * ssdn_ref[...]).astype(BF16)

    t_glob = sj * ts + lax.broadcasted_iota(jnp.int32, (ts, 1), 0)
    pooled = []
    for gi, win in enumerate(POOL_WINDOWS):
        pcols = slice(gi * POOL_GROUP_DIM, (gi + 1) * POOL_GROUP_DIM)
        acc = uext[:, pcols]
        u0 = acc[POOL_HIST:, :]
        span = 1
        while span < win:
            acc = acc + pltpu.roll(acc, span, axis=0)
            span *= 2
        acc = acc[POOL_HIST:, :]
        inv_cnt = 1.0 / jnp.minimum(t_glob + 1, win).astype(F32)
        pooled.append(_dot((acc * inv_cnt - u0).astype(BF16), poolw_ref[gi]))
    y_pool = (jnp.concatenate(pooled, axis=-1) * pools_ref[...]).astype(BF16)
    uext[0:POOL_HIST, :] = uext[ts:ts + POOL_HIST, :]

    o_ref[0] = x + _dot(y_ssd, wo_ref[0:SSD_DIM, :]) + _dot(y_pool, wo_ref[SSD_DIM:, :])


def _expand_matrix():
    e = np.zeros((LANES, 2 * SSD_DIM), np.float32)
    for k in range(6):
        for hd in range(SSD_HEADS):
            base = (k // 3) * SSD_DIM + hd * SSD_HEAD_DIM
            e[k * HEAD_SLOT + hd, base:base + SSD_HEAD_DIM] = 1.0
    return e


def _mix(x, gain, wz, wxbc, wdt, wu, conv_w, conv_b, dt_bias, a_log, d_skip_e, ssd_norm,
         pool_w, pool_scale, w_out):
    b, s, _ = x.shape
    ts = MIX_TS
    expand = jnp.asarray(_expand_matrix(), BF16)
    operands = (x, gain, wz, wxbc, wdt, wu, conv_w, conv_b, dt_bias, a_log, d_skip_e, ssd_norm,
                expand, pool_w, pool_scale, w_out)
    in_specs = [pl.BlockSpec((1, ts, D_MODEL), lambda i, j: (i, j, 0))]
    in_specs += [_resident(op.shape) for op in operands[1:]]
    return pl.pallas_call(
        _mix_kernel,
        out_shape=jax.ShapeDtypeStruct((b, s, D_MODEL), F32),
        grid=(b, s // ts),
        in_specs=in_specs,
        out_specs=pl.BlockSpec((1, ts, D_MODEL), lambda i, j: (i, j, 0)),
        scratch_shapes=[
            pltpu.VMEM((CONV_HIST + ts, CONV_DIM), F32),
            pltpu.VMEM((POOL_HIST + ts, POOL_DIM), F32),
            pltpu.VMEM((SSD_NGROUPS, SSD_STATE, GROUP_DIM), F32),
            pltpu.VMEM((ts, SSD_DIM), F32),
            pltpu.VMEM((ts, SSD_DIM), F32),
            pltpu.VMEM((ts, 2 * BC_DIM), F32),
            pltpu.VMEM((ts, 2 * SSD_DIM), F32),
            pltpu.VMEM((ts, SSD_DIM), F32),
        ],
        compiler_params=pltpu.CompilerParams(
            dimension_semantics=("arbitrary", "arbitrary"), vmem_limit_bytes=VMEM_LIMIT),
        name="mix",
    )(*operands)


def kernel(x, ffn1_norm, ffn1_w_gate, ffn1_w_up, ffn1_w_down, mix_norm, w_in, conv_w, conv_b,
           dt_bias, a_log, d_skip, ssd_norm, pool_w, pool_scale, w_out, ffn2_norm,
           ffn2_w_gate, ffn2_w_up, ffn2_w_down, final_norm):
    b, s, d = x.shape
    depth = ffn1_norm.shape[0]
    o1 = SSD_DIM
    o2 = o1 + CONV_DIM
    o3 = o2 + SSD_HEADS
    rep = LANES // SSD_HEADS
    row = lambda v: v.reshape(1, -1)
    ones = jnp.ones((1, d), F32)
    for i in range(depth):
        last = i == depth - 1
        x = _ffn(x.reshape(b * s, d), row(ffn1_norm[i]), ffn1_w_gate[i].astype(BF16),
                 ffn1_w_up[i].astype(BF16), ffn1_w_down[i].astype(BF16), ones,
                 final_norm=False, name="ffn1").reshape(b, s, d)
        wi = w_in[i]
        x = _mix(
            x, row(mix_norm[i]),
            wi[:, :o1].astype(BF16), wi[:, o1:o2].astype(BF16),
            jnp.tile(wi[:, o2:o3], (1, rep)).astype(BF16), wi[:, o3:].astype(BF16),
            conv_w[i], row(conv_b[i]),
            jnp.tile(row(dt_bias[i]), (1, rep)), jnp.tile(row(a_log[i]), (1, rep)),
            row(jnp.repeat(d_skip[i], SSD_HEAD_DIM)), row(ssd_norm[i]),
            pool_w[i].astype(BF16), row(pool_scale[i]), w_out[i].astype(BF16))
        x = _ffn(x.reshape(b * s, d), row(ffn2_norm[i]), ffn2_w_gate[i].astype(BF16),
                 ffn2_w_up[i].astype(BF16), ffn2_w_down[i].astype(BF16),
                 row(final_norm) if last else ones,
                 final_norm=last, name="ffn2").reshape(b, s, d)
    return x
```

```python
import functools

import numpy as np
import jax
import jax.numpy as jnp
from jax import lax
from jax.experimental import pallas as pl
from jax.experimental.pallas import tpu as pltpu

D_MODEL = 1024
CHUNK = 64
SSD_DIM = 1024
SSD_HEAD_DIM = 64
SSD_HEADS = 16
SSD_NGROUPS = 2
SSD_STATE = 128
SSD_CONV = 4
POOL_DIM = 1024
POOL_WINDOWS = (2, 4, 8, 16)
POOL_GROUP_DIM = 256
CONV_DIM = SSD_DIM + 2 * SSD_NGROUPS * SSD_STATE
D_FF = 2816
NORM_EPS = 1e-6

GROUP_DIM = SSD_DIM // SSD_NGROUPS
BC_DIM = SSD_NGROUPS * SSD_STATE
LANES = 128
SUBLANES = 8
HEAD_SLOT = 16
CONV_HIST = SUBLANES
POOL_HIST = 16

FFN_TM = 512
FFN_TF = 256
MIX_TS = 512
VMEM_LIMIT = 56 * 1024 * 1024

F32 = jnp.float32
BF16 = jnp.bfloat16
U32 = jnp.uint32


def _silu(v):
    return v * jax.nn.sigmoid(v)


def _rms(v, gain):
    return v * lax.rsqrt(jnp.mean(v * v, axis=-1, keepdims=True) + NORM_EPS) * gain


def _dot(a, b):
    return jnp.dot(a, b, preferred_element_type=F32)


def _pack_rows(w):
    k, n = w.shape
    return lax.bitcast_convert_type(w.astype(BF16).reshape(k // 2, 2, n).transpose(0, 2, 1), U32)


def _bf16(packed):
    return pltpu.bitcast(packed, BF16)


def _ffn_kernel(x_ref, g_ref, wg_ref, wu_ref, wd_ref, pg_ref, *rest, post):
    if post == "both":
        o_ref, h_ref, act_ref = rest
    else:
        o_ref, act_ref = rest
    x = x_ref[...]
    h = _rms(x, g_ref[...]).astype(BF16)
    for j in range(D_FF // FFN_TF):
        cols = slice(j * FFN_TF, (j + 1) * FFN_TF)
        gate = _dot(h, _bf16(wg_ref[:, cols]))
        up = _dot(h, _bf16(wu_ref[:, cols]))
        act_ref[:, cols] = (_silu(gate) * up).astype(BF16)
    y = x + 0.5 * _dot(act_ref[...], _bf16(wd_ref[...]))
    if post == "norm":
        o_ref[...] = _rms(y, pg_ref[...])
    else:
        o_ref[...] = y
        h_ref[...] = pltpu.bitcast(_rms(y, pg_ref[...]).astype(BF16), U32)


def _resident(shape):
    return pl.BlockSpec(shape, lambda *_: (0,) * len(shape), pipeline_mode=pl.Buffered(1))


def _ffn(x2d, gain, w_gate, w_up, w_down, post_gain, *, post, name):
    m = x2d.shape[0]
    tile = pl.BlockSpec((FFN_TM, D_MODEL), lambda i: (i, 0))
    out_shape = jax.ShapeDtypeStruct((m, D_MODEL), F32)
    out_specs = tile
    if post == "both":
        out_shape = (out_shape, jax.ShapeDtypeStruct((m // 2, D_MODEL), U32))
        out_specs = (tile, pl.BlockSpec((FFN_TM // 2, D_MODEL), lambda i: (i, 0)))
    return pl.pallas_call(
        functools.partial(_ffn_kernel, post=post),
        out_shape=out_shape,
        grid=(m // FFN_TM,),
        in_specs=[
            tile,
            _resident((1, D_MODEL)),
            _resident((D_MODEL // 2, D_FF)),
            _resident((D_MODEL // 2, D_FF)),
            _resident((D_FF // 2, D_MODEL)),
            _resident((1, D_MODEL)),
        ],
        out_specs=out_specs,
        scratch_shapes=[pltpu.VMEM((FFN_TM, D_FF), BF16)],
        compiler_params=pltpu.CompilerParams(
            dimension_semantics=("arbitrary",), vmem_limit_bytes=VMEM_LIMIT),
        name=name,
    )(x2d, gain, _pack_rows(w_gate), _pack_rows(w_up), _pack_rows(w_down), post_gain)


def _split3(v):
    hi = v.astype(BF16).astype(F32)
    r = v - hi
    mid = r.astype(BF16).astype(F32)
    return hi, mid, r - mid


def _mix_kernel(h_ref, x_ref, wz_ref, wxbc_ref, wdt_ref, wu_ref, convw_ref, convb_ref,
                dtb_ref, alog_ref, dskip_ref, ssdn_ref, expand_ref, poolw_ref, pools_ref,
                wo_ref, o_ref,
                xext, uext, state, z_s, xs_s, bc_s, ee_s, y_s):
    ts = MIX_TS
    sj = pl.program_id(1)

    @pl.when(sj == 0)
    def _():
        xext[0:CONV_HIST, :] = jnp.zeros((CONV_HIST, CONV_DIM), F32)
        uext[0:POOL_HIST, :] = jnp.zeros((POOL_HIST, POOL_DIM), F32)
        state[...] = jnp.zeros_like(state)

    h = _bf16(h_ref[0])
    z_s[...] = _dot(h, _bf16(wz_ref[...]))
    xext[CONV_HIST:CONV_HIST + ts, :] = _dot(h, _bf16(wxbc_ref[...]))
    uext[POOL_HIST:POOL_HIST + ts, :] = _dot(h, _bf16(wu_ref[...]))

    dt_raw = _dot(h, _bf16(wdt_ref[...])) + dtb_ref[...]
    dt = jnp.maximum(dt_raw, 0.0) + jnp.log1p(jnp.exp(-jnp.abs(dt_raw)))
    cs = dt * (-jnp.exp(alog_ref[...]))
    row_in_chunk = lax.broadcasted_iota(jnp.int32, (ts, LANES), 0) % CHUNK
    shift = 1
    while shift < CHUNK:
        cs = cs + jnp.where(row_in_chunk >= shift, pltpu.roll(cs, shift, axis=0), 0.0)
        shift *= 2

    slot = lax.broadcasted_iota(jnp.int32, (ts, LANES), 1) // HEAD_SLOT
    terms = _split3(dt) + _split3(cs)
    packed = jnp.zeros((ts, LANES), F32)
    for k, term in enumerate(terms):
        packed = jnp.where(slot == k, term, packed)
    ee_s[...] = _dot(packed.astype(BF16), _bf16(expand_ref[...]))

    conv = convb_ref[...] + convw_ref[SSD_CONV - 1:SSD_CONV, :] * xext[CONV_HIST:CONV_HIST + ts, :]
    for k in range(SSD_CONV - 1):
        off = CONV_HIST - (SSD_CONV - 1) + k
        conv = conv + convw_ref[k:k + 1, :] * xext[off:off + ts, :]
    conv = _silu(conv)
    xs_s[...] = conv[:, :SSD_DIM]
    bc_s[...] = conv[:, SSD_DIM:]
    xext[0:CONV_HIST, :] = xext[ts:ts + CONV_HIST, :]

    sub = lax.broadcasted_iota(jnp.int32, (CHUNK, LANES), 0)
    lane = lax.broadcasted_iota(jnp.int32, (CHUNK, LANES), 1)
    diag = sub == lane % CHUNK
    causal = sub >= lane % CHUNK
    left = lane < CHUNK
    for c in range(ts // CHUNK):
        rows = slice(c * CHUNK, (c + 1) * CHUNK)
        for g in range(SSD_NGROUPS):
            gcols = slice(g * GROUP_DIM, (g + 1) * GROUP_DIM)
            xg = xs_s[rows, gcols]
            b_mat = bc_s[rows, g * SSD_STATE:(g + 1) * SSD_STATE]
            c_mat = bc_s[rows, BC_DIM + g * SSD_STATE:BC_DIM + (g + 1) * SSD_STATE]
            dt_e = ee_s[rows, gcols]
            cs_e = ee_s[rows, SSD_DIM + g * GROUP_DIM:SSD_DIM + (g + 1) * GROUP_DIM]
            cs_last = cs_e[CHUNK - 1:CHUNK, :]
            xdt = xg * dt_e
            xdt_b = xdt.astype(BF16)
            x_decayed = (xdt * jnp.exp(cs_last - cs_e)).astype(BF16)
            c_b = c_mat.astype(BF16)
            b_b = b_mat.astype(BF16)
            s_prev = state[g]
            y_off = _dot(c_b, s_prev.astype(BF16)) * jnp.exp(cs_e)
            s_new = _dot(b_mat.T.astype(BF16), x_decayed)
            state[g] = jnp.exp(cs_last) * s_prev + s_new
            cb2 = lax.dot_general(c_b, jnp.concatenate([b_b, b_b], axis=0),
                                  (((1,), (1,)), ((), ())), preferred_element_type=F32)
            for p in range(GROUP_DIM // LANES):
                pc = slice(p * LANES, (p + 1) * LANES)
                col = cs_e[:, pc]
                rowv = jnp.sum(jnp.where(diag, col, 0.0), axis=0, keepdims=True)
                decay = jnp.exp(jnp.where(causal, col - rowv, -jnp.inf))
                g_pair = (cb2 * decay).astype(BF16)
                xp = xdt_b[:, pc]
                zero = jnp.zeros_like(xp)
                block_diag = jnp.concatenate(
                    [jnp.where(left, xp, zero), jnp.where(left, zero, xp)], axis=0)
                oc = slice(g * GROUP_DIM + p * LANES, g * GROUP_DIM + (p + 1) * LANES)
                y_s[rows, oc] = (_dot(g_pair, block_diag) + y_off[:, pc]
                                 + dskip_ref[:, oc] * xg[:, pc])

    yg = y_s[...] * _silu(z_s[...])
    normed = []
    for g in range(SSD_NGROUPS):
        seg = yg[:, g * GROUP_DIM:(g + 1) * GROUP_DIM]
        normed.append(seg * lax.rsqrt(jnp.mean(seg * seg, axis=-1, keepdims=True) + NORM_EPS))
    y_ssd = (jnp.concatenate(normed, axis=-1) * ssdn_ref[...]).astype(BF16)

    t_glob = sj * ts + lax.broadcasted_iota(jnp.int32, (ts, 1), 0)
    pooled = []
    for gi, win in enumerate(POOL_WINDOWS):
        pcols = slice(gi * POOL_GROUP_DIM, (gi + 1) * POOL_GROUP_DIM)
        acc = uext[:, pcols]
        u0 = acc[POOL_HIST:, :]
        span = 1
        while span < win:
            acc = acc + pltpu.roll(acc, span, axis=0)
            span *= 2
        acc = acc[POOL_HIST:, :]
        inv_cnt = 1.0 / jnp.minimum(t_glob + 1, win).astype(F32)
        pooled.append(_dot((acc * inv_cnt - u0).astype(BF16), _bf16(poolw_ref[gi])))
    y_pool = (jnp.concatenate(pooled, axis=-1) * pools_ref[...]).astype(BF16)
    uext[0:POOL_HIST, :] = uext[ts:ts + POOL_HIST, :]

    o_ref[0] = (x_ref[0] + _dot(y_ssd, _bf16(wo_ref[0:SSD_DIM // 2, :]))
                + _dot(y_pool, _bf16(wo_ref[SSD_DIM // 2:, :])))


def _expand_matrix():
    e = np.zeros((LANES, 2 * SSD_DIM), np.uint32)
    bf16_one = 0x3F80
    for k in range(6):
        for hd in range(SSD_HEADS):
            base = (k // 3) * SSD_DIM + hd * SSD_HEAD_DIM
            e[k * HEAD_SLOT + hd, base:base + SSD_HEAD_DIM] = bf16_one
    return e[0::2] | (e[1::2] << 16)


def _mix(x, h_packed, wz, wxbc, wdt, wu, conv_w, conv_b, dt_bias, a_log, d_skip_e, ssd_norm,
         pool_w, pool_scale, w_out):
    b, s, d = x.shape
    ts = MIX_TS
    expand = jnp.asarray(_expand_matrix())
    pool_packed = jnp.stack([_pack_rows(pool_w[g]) for g in range(pool_w.shape[0])])
    weights = (_pack_rows(wz), _pack_rows(wxbc), _pack_rows(wdt), _pack_rows(wu), conv_w, conv_b,
               dt_bias, a_log, d_skip_e, ssd_norm, expand, pool_packed, pool_scale,
               _pack_rows(w_out))
    in_specs = [pl.BlockSpec((1, ts // 2, d), lambda i, j: (i, j, 0)),
                pl.BlockSpec((1, ts, d), lambda i, j: (i, j, 0))]
    in_specs += [_resident(w.shape) for w in weights]
    return pl.pallas_call(
        _mix_kernel,
        out_shape=jax.ShapeDtypeStruct((b, s, d), F32),
        grid=(b, s // ts),
        in_specs=in_specs,
        out_specs=pl.BlockSpec((1, ts, d), lambda i, j: (i, j, 0)),
        scratch_shapes=[
            pltpu.VMEM((CONV_HIST + ts, CONV_DIM), F32),
            pltpu.VMEM((POOL_HIST + ts, POOL_DIM), F32),
            pltpu.VMEM((SSD_NGROUPS, SSD_STATE, GROUP_DIM), F32),
            pltpu.VMEM((ts, SSD_DIM), F32),
            pltpu.VMEM((ts, SSD_DIM), F32),
            pltpu.VMEM((ts, 2 * BC_DIM), F32),
            pltpu.VMEM((ts, 2 * SSD_DIM), F32),
            pltpu.VMEM((ts, SSD_DIM), F32),
        ],
        compiler_params=pltpu.CompilerParams(
            dimension_semantics=("arbitrary", "arbitrary"), vmem_limit_bytes=VMEM_LIMIT),
        name="mix",
    )(h_packed, x, *weights)


def kernel(x, ffn1_norm, ffn1_w_gate, ffn1_w_up, ffn1_w_down, mix_norm, w_in, conv_w, conv_b,
           dt_bias, a_log, d_skip, ssd_norm, pool_w, pool_scale, w_out, ffn2_norm,
           ffn2_w_gate, ffn2_w_up, ffn2_w_down, final_norm):
    b, s, d = x.shape
    assert ffn1_norm.shape[0] == 1, "one macaron block"
    o1 = SSD_DIM
    o2 = o1 + CONV_DIM
    o3 = o2 + SSD_HEADS
    rep = LANES // SSD_HEADS
    row = lambda v: v.reshape(1, -1)
    x2d, h2d = _ffn(x.reshape(b * s, d), row(ffn1_norm[0]), ffn1_w_gate[0], ffn1_w_up[0],
                    ffn1_w_down[0], row(mix_norm[0]), post="both", name="ffn1")
    wi = w_in[0]
    x3d = _mix(
        x2d.reshape(b, s, d), h2d.reshape(b, s // 2, d),
        wi[:, :o1], wi[:, o1:o2], jnp.tile(wi[:, o2:o3], (1, rep)), wi[:, o3:],
        conv_w[0], row(conv_b[0]),
        jnp.tile(row(dt_bias[0]), (1, rep)), jnp.tile(row(a_log[0]), (1, rep)),
        row(jnp.repeat(d_skip[0], SSD_HEAD_DIM)), row(ssd_norm[0]),
        pool_w[0], row(pool_scale[0]), w_out[0])
    out = _ffn(x3d.reshape(b * s, d), row(ffn2_norm[0]), ffn2_w_gate[0], ffn2_w_up[0],
               ffn2_w_down[0], row(final_norm), post="norm", name="ffn2")
    return out.reshape(b, s, d)
```

```python
import functools

import numpy as np
import jax
import jax.numpy as jnp
from jax import lax
from jax.experimental import pallas as pl
from jax.experimental.pallas import tpu as pltpu

D_MODEL = 1024
CHUNK = 64
SSD_DIM = 1024
SSD_HEAD_DIM = 64
SSD_HEADS = 16
SSD_NGROUPS = 2
SSD_STATE = 128
SSD_CONV = 4
POOL_DIM = 1024
POOL_WINDOWS = (2, 4, 8, 16)
POOL_GROUP_DIM = 256
CONV_DIM = SSD_DIM + 2 * SSD_NGROUPS * SSD_STATE
D_FF = 2816
NORM_EPS = 1e-6

GROUP_DIM = SSD_DIM // SSD_NGROUPS
BC_DIM = SSD_NGROUPS * SSD_STATE
LANES = 128
SUBLANES = 8
HEAD_SLOT = 16
CONV_HIST = SUBLANES
POOL_HIST = 16

FFN_TM = 512
FFN_TF = 256
FFN_NW = D_FF // FFN_TF
MIX_TS = 512
VMEM_LIMIT = 56 * 1024 * 1024

F32 = jnp.float32
BF16 = jnp.bfloat16


def _silu(v):
    return v * jax.nn.sigmoid(v)


def _rms(v, gain):
    return v * lax.rsqrt(jnp.mean(v * v, axis=-1, keepdims=True) + NORM_EPS) * gain


def _dot(a, b):
    return jnp.dot(a, b, preferred_element_type=F32)


def _ffn_kernel(x_ref, g_ref, wg_ref, wu_ref, wd_ref, fg_ref, o_ref,
                wg_s, wu_s, wd_s, act_ref, *, final_norm):
    i = pl.program_id(0)

    @pl.when(i < FFN_NW)
    def _():
        wg_s[i] = wg_ref[...].astype(BF16)
        wu_s[i] = wu_ref[...].astype(BF16)
        wd_s[pl.ds(pl.multiple_of(i * FFN_TF, FFN_TF), FFN_TF), :] = wd_ref[...].astype(BF16)

    @pl.when(i >= FFN_NW)
    def _():
        x = x_ref[...]
        h = _rms(x, g_ref[...]).astype(BF16)
        for j in range(FFN_NW):
            gate = _dot(h, wg_s[j])
            up = _dot(h, wu_s[j])
            act_ref[:, j * FFN_TF:(j + 1) * FFN_TF] = (_silu(gate) * up).astype(BF16)
        y = x + 0.5 * _dot(act_ref[...], wd_s[...])
        if final_norm:
            y = _rms(y, fg_ref[...])
        o_ref[...] = y


def _resident(shape):
    return pl.BlockSpec(shape, lambda *_: (0,) * len(shape), pipeline_mode=pl.Buffered(1))


def _ffn(x2d, gain, w_gate, w_up, w_down, final_gain, *, final_norm, name):
    m = x2d.shape[0]
    wchunk = lambda i: jnp.minimum(i, FFN_NW - 1)
    tile = pl.BlockSpec((FFN_TM, D_MODEL), lambda i: (jnp.maximum(i - FFN_NW, 0), 0))
    return pl.pallas_call(
        functools.partial(_ffn_kernel, final_norm=final_norm),
        out_shape=jax.ShapeDtypeStruct((m, D_MODEL), F32),
        grid=(FFN_NW + m // FFN_TM,),
        in_specs=[
            tile,
            _resident((1, D_MODEL)),
            pl.BlockSpec((D_MODEL, FFN_TF), lambda i: (0, wchunk(i))),
            pl.BlockSpec((D_MODEL, FFN_TF), lambda i: (0, wchunk(i))),
            pl.BlockSpec((FFN_TF, D_MODEL), lambda i: (wchunk(i), 0)),
            _resident((1, D_MODEL)),
        ],
        out_specs=tile,
        scratch_shapes=[
            pltpu.VMEM((FFN_NW, D_MODEL, FFN_TF), BF16),
            pltpu.VMEM((FFN_NW, D_MODEL, FFN_TF), BF16),
            pltpu.VMEM((D_FF, D_MODEL), BF16),
            pltpu.VMEM((FFN_TM, D_FF), BF16),
        ],
        compiler_params=pltpu.CompilerParams(
            dimension_semantics=("arbitrary",), vmem_limit_bytes=VMEM_LIMIT),
        name=name,
    )(x2d, gain, w_gate, w_up, w_down, final_gain)


def _split3(v):
    hi = v.astype(BF16).astype(F32)
    r = v - hi
    mid = r.astype(BF16).astype(F32)
    return hi, mid, r - mid


def _mix_kernel(x_ref, g_ref, wz_ref, wxbc_ref, wdt_ref, wu_ref, convw_ref, convb_ref,
                dtb_ref, alog_ref, dskip_ref, ssdn_ref, expand_ref, poolw_ref, pools_ref,
                wo_ref, o_ref,
                xext, uext, state, z_s, xs_s, bc_s, ee_s, y_s):
    ts = MIX_TS
    sj = pl.program_id(1)

    @pl.when(sj == 0)
    def _():
        xext[0:CONV_HIST, :] = jnp.zeros((CONV_HIST, CONV_DIM), F32)
        uext[0:POOL_HIST, :] = jnp.zeros((POOL_HIST, POOL_DIM), F32)
        state[...] = jnp.zeros_like(state)

    x = x_ref[0]
    h = _rms(x, g_ref[...]).astype(BF16)
    z_s[...] = _dot(h, wz_ref[...])
    xext[CONV_HIST:CONV_HIST + ts, :] = _dot(h, wxbc_ref[...])
    uext[POOL_HIST:POOL_HIST + ts, :] = _dot(h, wu_ref[...])

    dt_raw = _dot(h, wdt_ref[...]) + dtb_ref[...]
    dt = jnp.maximum(dt_raw, 0.0) + jnp.log1p(jnp.exp(-jnp.abs(dt_raw)))
    cs = dt * (-jnp.exp(alog_ref[...]))
    row_in_chunk = lax.broadcasted_iota(jnp.int32, (ts, LANES), 0) % CHUNK
    shift = 1
    while shift < CHUNK:
        cs = cs + jnp.where(row_in_chunk >= shift, pltpu.roll(cs, shift, axis=0), 0.0)
        shift *= 2

    slot = lax.broadcasted_iota(jnp.int32, (ts, LANES), 1) // HEAD_SLOT
    terms = _split3(dt) + _split3(cs)
    packed = jnp.zeros((ts, LANES), F32)
    for k, term in enumerate(terms):
        packed = jnp.where(slot == k, term, packed)
    ee_s[...] = _dot(packed.astype(BF16), expand_ref[...])

    conv = convb_ref[...] + convw_ref[SSD_CONV - 1:SSD_CONV, :] * xext[CONV_HIST:CONV_HIST + ts, :]
    for k in range(SSD_CONV - 1):
        off = CONV_HIST - (SSD_CONV - 1) + k
        conv = conv + convw_ref[k:k + 1, :] * xext[off:off + ts, :]
    conv = _silu(conv)
    xs_s[...] = conv[:, :SSD_DIM]
    bc_s[...] = conv[:, SSD_DIM:]
    xext[0:CONV_HIST, :] = xext[ts:ts + CONV_HIST, :]

    sub = lax.broadcasted_iota(jnp.int32, (CHUNK, LANES), 0)
    lane = lax.broadcasted_iota(jnp.int32, (CHUNK, LANES), 1)
    diag = sub == lane % CHUNK
    causal = sub >= lane % CHUNK
    left = lane < CHUNK
    for c in range(ts // CHUNK):
        rows = slice(c * CHUNK, (c + 1) * CHUNK)
        for g in range(SSD_NGROUPS):
            gcols = slice(g * GROUP_DIM, (g + 1) * GROUP_DIM)
            xg = xs_s[rows, gcols]
            b_mat = bc_s[rows, g * SSD_STATE:(g + 1) * SSD_STATE]
            c_mat = bc_s[rows, BC_DIM + g * SSD_STATE:BC_DIM + (g + 1) * SSD_STATE]
            dt_e = ee_s[rows, gcols]
            cs_e = ee_s[rows, SSD_DIM + g * GROUP_DIM:SSD_DIM + (g + 1) * GROUP_DIM]
            cs_last = cs_e[CHUNK - 1:CHUNK, :]
            xdt = xg * dt_e
            xdt_b = xdt.astype(BF16)
            x_decayed = (xdt * jnp.exp(cs_last - cs_e)).astype(BF16)
            c_b = c_mat.astype(BF16)
            b_b = b_mat.astype(BF16)
            s_prev = state[g]
            y_off = _dot(c_b, s_prev.astype(BF16)) * jnp.exp(cs_e)
            s_new = _dot(b_mat.T.astype(BF16), x_decayed)
            state[g] = jnp.exp(cs_last) * s_prev + s_new
            cb2 = lax.dot_general(c_b, jnp.concatenate([b_b, b_b], axis=0),
                                  (((1,), (1,)), ((), ())), preferred_element_type=F32)
            for p in range(GROUP_DIM // LANES):
                pc = slice(p * LANES, (p + 1) * LANES)
                col = cs_e[:, pc]
                rowv = jnp.sum(jnp.where(diag, col, 0.0), axis=0, keepdims=True)
                decay = jnp.exp(jnp.where(causal, col - rowv, -jnp.inf))
                g_pair = (cb2 * decay).astype(BF16)
                xp = xdt_b[:, pc]
                zero = jnp.zeros_like(xp)
                block_diag = jnp.concatenate(
                    [jnp.where(left, xp, zero), jnp.where(left, zero, xp)], axis=0)
                oc = slice(g * GROUP_DIM + p * LANES, g * GROUP_DIM + (p + 1) * LANES)
                y_s[rows, oc] = (_dot(g_pair, block_diag) + y_off[:, pc]
                                 + dskip_ref[:, oc] * xg[:, pc])

    yg = y_s[...] * _silu(z_s[...])
    normed = []
    for g in range(SSD_NGROUPS):
        seg = yg[:, g * GROUP_DIM:(g + 1) * GROUP_DIM]
        normed.append(seg * lax.rsqrt(jnp.mean(seg * seg, axis=-1, keepdims=True) + NORM_EPS))
    y_ssd = (jnp.concatenate(normed, axis=-1) * ssdn_ref[...]).astype(BF16)

    t_glob = sj * ts + lax.broadcasted_iota(jnp.int32, (ts, 1), 0)
    pooled = []
    for gi, win in enumerate(POOL_WINDOWS):
        pcols = slice(gi * POOL_GROUP_DIM, (gi + 1) * POOL_GROUP_DIM)
        acc = uext[:, pcols]
        u0 = acc[POOL_HIST:, :]
        span = 1
        while span < win:
            acc = acc + pltpu.roll(acc, span, axis=0)
            span *= 2
        acc = acc[POOL_HIST:, :]
        inv_cnt = 1.0 / jnp.minimum(t_glob + 1, win).astype(F32)
        pooled.append(_dot((acc * inv_cnt - u0).astype(BF16), poolw_ref[gi]))
    y_pool = (jnp.concatenate(pooled, axis=-1) * pools_ref[...]).astype(BF16)
    uext[0:POOL_HIST, :] = uext[ts:ts + POOL_HIST, :]

    o_ref[0] = x + _dot(y_ssd, wo_ref[0:SSD_DIM, :]) + _dot(y_pool, wo_ref[SSD_DIM:, :])


def _expand_matrix():
    e = np.zeros((LANES, 2 * SSD_DIM), np.float32)
    for k in range(6):
        for hd in range(SSD_HEADS):
            base = (k // 3) * SSD_DIM + hd * SSD_HEAD_DIM
            e[k * HEAD_SLOT + hd, base:base + SSD_HEAD_DIM] = 1.0
    return e


def _mix(x, gain, wz, wxbc, wdt, wu, conv_w, conv_b, dt_bias, a_log, d_skip_e, ssd_norm,
         pool_w, pool_scale, w_out):
    b, s, _ = x.shape
    ts = MIX_TS
    expand = jnp.asarray(_expand_matrix(), BF16)
    operands = (x, gain, wz, wxbc, wdt, wu, conv_w, conv_b, dt_bias, a_log, d_skip_e, ssd_norm,
                expand, pool_w, pool_scale, w_out)
    in_specs = [pl.BlockSpec((1, ts, D_MODEL), lambda i, j: (i, j, 0))]
    in_specs += [_resident(op.shape) for op in operands[1:]]
    return pl.pallas_call(
        _mix_kernel,
        out_shape=jax.ShapeDtypeStruct((b, s, D_MODEL), F32),
        grid=(b, s // ts),
        in_specs=in_specs,
        out_specs=pl.BlockSpec((1, ts, D_MODEL), lambda i, j: (i, j, 0)),
        scratch_shapes=[
            pltpu.VMEM((CONV_HIST + ts, CONV_DIM), F32),
            pltpu.VMEM((POOL_HIST + ts, POOL_DIM), F32),
            pltpu.VMEM((SSD_NGROUPS, SSD_STATE, GROUP_DIM), F32),
            pltpu.VMEM((ts, SSD_DIM), F32),
            pltpu.VMEM((ts, SSD_DIM), F32),
            pltpu.VMEM((ts, 2 * BC_DIM), F32),
            pltpu.VMEM((ts, 2 * SSD_DIM), F32),
            pltpu.VMEM((ts, SSD_DIM), F32),
        ],
        compiler_params=pltpu.CompilerParams(
            dimension_semantics=("arbitrary", "arbitrary"), vmem_limit_bytes=VMEM_LIMIT),
        name="mix",
    )(*operands)


def kernel(x, ffn1_norm, ffn1_w_gate, ffn1_w_up, ffn1_w_down, mix_norm, w_in, conv_w, conv_b,
           dt_bias, a_log, d_skip, ssd_norm, pool_w, pool_scale, w_out, ffn2_norm,
           ffn2_w_gate, ffn2_w_up, ffn2_w_down, final_norm):
    b, s, d = x.shape
    depth = ffn1_norm.shape[0]
    o1 = SSD_DIM
    o2 = o1 + CONV_DIM
    o3 = o2 + SSD_HEADS
    rep = LANES // SSD_HEADS
    row = lambda v: v.reshape(1, -1)
    ones = jnp.ones((1, d), F32)
    for i in range(depth):
        last = i == depth - 1
        x = _ffn(x.reshape(b * s, d), row(ffn1_norm[i]), ffn1_w_gate[i], ffn1_w_up[i],
                 ffn1_w_down[i], ones, final_norm=False, name="ffn1").reshape(b, s, d)
        wi = w_in[i]
        x = _mix(
            x, row(mix_norm[i]),
            wi[:, :o1].astype(BF16), wi[:, o1:o2].astype(BF16),
            jnp.tile(wi[:, o2:o3], (1, rep)).astype(BF16), wi[:, o3:].astype(BF16),
            conv_w[i], row(conv_b[i]),
            jnp.tile(row(dt_bias[i]), (1, rep)), jnp.tile(row(a_log[i]), (1, rep)),
            row(jnp.repeat(d_skip[i], SSD_HEAD_DIM)), row(ssd_norm[i]),
            pool_w[i].astype(BF16), row(pool_scale[i]), w_out[i].astype(BF16))
        x = _ffn(x.reshape(b * s, d), row(ffn2_norm[i]), ffn2_w_gate[i], ffn2_w_up[i],
                 ffn2_w_down[i], row(final_norm) if last else ones,
                 final_norm=last, name="ffn2").reshape(b, s, d)
    return x
```

```python
import functools

import numpy as np
import jax
import jax.numpy as jnp
from jax import lax
from jax.experimental import pallas as pl
from jax.experimental.pallas import tpu as pltpu

D_MODEL = 1024
CHUNK = 64
SSD_DIM = 1024
SSD_HEAD_DIM = 64
SSD_HEADS = 16
SSD_NGROUPS = 2
SSD_STATE = 128
SSD_CONV = 4
POOL_DIM = 1024
POOL_WINDOWS = (2, 4, 8, 16)
POOL_GROUP_DIM = 256
CONV_DIM = SSD_DIM + 2 * SSD_NGROUPS * SSD_STATE
D_FF = 2816
NORM_EPS = 1e-6

GROUP_DIM = SSD_DIM // SSD_NGROUPS
BC_DIM = SSD_NGROUPS * SSD_STATE
LANES = 128
SUBLANES = 8
HEAD_SLOT = 16
CONV_HIST = SUBLANES
POOL_HIST = 16

FFN_TM = 1024
FFN_TF = 256
FFN_NW = D_FF // FFN_TF
MIX_TS = 512
MIX_WCOLS = 256
MIX_NW = 11
VMEM_LIMIT = 56 * 1024 * 1024

F32 = jnp.float32
BF16 = jnp.bfloat16


def _silu(v):
    return v * jax.nn.sigmoid(v)


def _rms(v, gain):
    return v * lax.rsqrt(jnp.mean(v * v, axis=-1, keepdims=True) + NORM_EPS) * gain


def _dot(a, b):
    return jnp.dot(a, b, preferred_element_type=F32)


def _ffn_kernel(x_ref, g_ref, wg_ref, wu_ref, wd_ref, fg_ref, o_ref,
                wg_s, wu_s, wd_s, act_ref, *, final_norm):
    i = pl.program_id(0)

    @pl.when(i < FFN_NW)
    def _():
        wg_s[i] = wg_ref[...].astype(BF16)
        wu_s[i] = wu_ref[...].astype(BF16)
        wd_s[pl.ds(pl.multiple_of(i * FFN_TF, FFN_TF), FFN_TF), :] = wd_ref[...].astype(BF16)

    @pl.when(i >= FFN_NW)
    def _():
        x = x_ref[...]
        h = _rms(x, g_ref[...]).astype(BF16)
        for j in range(FFN_NW):
            gate = _dot(h, wg_s[j])
            up = _dot(h, wu_s[j])
            act_ref[:, j * FFN_TF:(j + 1) * FFN_TF] = (_silu(gate) * up).astype(BF16)
        y = x + 0.5 * _dot(act_ref[...], wd_s[...])
        if final_norm:
            y = _rms(y, fg_ref[...])
        o_ref[...] = y


def _resident(shape):
    return pl.BlockSpec(shape, lambda *_: (0,) * len(shape), pipeline_mode=pl.Buffered(1))


def _ffn(x2d, gain, w_gate, w_up, w_down, final_gain, *, final_norm, name):
    m = x2d.shape[0]
    wchunk = lambda i: jnp.minimum(i, FFN_NW - 1)
    tile = pl.BlockSpec((FFN_TM, D_MODEL), lambda i: (jnp.maximum(i - FFN_NW, 0), 0))
    return pl.pallas_call(
        functools.partial(_ffn_kernel, final_norm=final_norm),
        out_shape=jax.ShapeDtypeStruct((m, D_MODEL), F32),
        grid=(FFN_NW + m // FFN_TM,),
        in_specs=[
            tile,
            _resident((1, D_MODEL)),
            pl.BlockSpec((D_MODEL, FFN_TF), lambda i: (0, wchunk(i))),
            pl.BlockSpec((D_MODEL, FFN_TF), lambda i: (0, wchunk(i))),
            pl.BlockSpec((FFN_TF, D_MODEL), lambda i: (wchunk(i), 0)),
            _resident((1, D_MODEL)),
        ],
        out_specs=tile,
        scratch_shapes=[
            pltpu.VMEM((FFN_NW, D_MODEL, FFN_TF), BF16),
            pltpu.VMEM((FFN_NW, D_MODEL, FFN_TF), BF16),
            pltpu.VMEM((D_FF, D_MODEL), BF16),
            pltpu.VMEM((FFN_TM, D_FF), BF16),
        ],
        compiler_params=pltpu.CompilerParams(
            dimension_semantics=("arbitrary",), vmem_limit_bytes=VMEM_LIMIT),
        name=name,
    )(x2d, gain, w_gate, w_up, w_down, final_gain)


def _split3(v):
    hi = v.astype(BF16).astype(F32)
    r = v - hi
    mid = r.astype(BF16).astype(F32)
    return hi, mid, r - mid


def _mix_kernel(x_ref, g_ref, win_ref, wuf_ref, wof_ref, poolf_ref, convw_ref, convb_ref,
                dtb_ref, alog_ref, dskip_ref, ssdn_ref, expand_ref, pools_ref, o_ref,
                wz_ref, wxbc_ref, wdt_ref, wu_ref, wo_ref, poolw_ref,
                xext, uext, state, z_s, xs_s, bc_s, ee_s, y_s, *, tiles_per_seq):
    i = pl.program_id(0)
    z_blocks = SSD_DIM // MIX_WCOLS
    xbc_blocks = CONV_DIM // MIX_WCOLS

    for j in range(MIX_NW):
        @pl.when(i == j)
        def _(j=j):
            blk = win_ref[...]
            if j < z_blocks:
                wz_ref[:, j * MIX_WCOLS:(j + 1) * MIX_WCOLS] = blk.astype(BF16)
            elif j < z_blocks + xbc_blocks:
                k = j - z_blocks
                wxbc_ref[:, k * MIX_WCOLS:(k + 1) * MIX_WCOLS] = blk.astype(BF16)
            else:
                lane = lax.broadcasted_iota(jnp.int32, (D_MODEL, LANES), 1)
                rep = jnp.where(lane < SSD_HEADS, blk[:, :LANES], 0.0)
                width = SSD_HEADS
                while width < LANES:
                    rep = rep + pltpu.roll(rep, width, axis=1)
                    width *= 2
                wdt_ref[...] = rep.astype(BF16)
            if j < POOL_DIM // MIX_WCOLS:
                wu_ref[:, j * MIX_WCOLS:(j + 1) * MIX_WCOLS] = wuf_ref[...].astype(BF16)
            if j < (SSD_DIM + POOL_DIM) // MIX_WCOLS:
                wo_ref[j * MIX_WCOLS:(j + 1) * MIX_WCOLS, :] = wof_ref[...].astype(BF16)
            if j < len(POOL_WINDOWS):
                poolw_ref[j] = poolf_ref[0].astype(BF16)

    @pl.when(i >= MIX_NW)
    def _():
        _mix_tile(x_ref, g_ref, wz_ref, wxbc_ref, wdt_ref, wu_ref, convw_ref, convb_ref,
                  dtb_ref, alog_ref, dskip_ref, ssdn_ref, expand_ref, poolw_ref, pools_ref,
                  wo_ref, o_ref, xext, uext, state, z_s, xs_s, bc_s, ee_s, y_s,
                  (i - MIX_NW) % tiles_per_seq)


def _mix_tile(x_ref, g_ref, wz_ref, wxbc_ref, wdt_ref, wu_ref, convw_ref, convb_ref,
              dtb_ref, alog_ref, dskip_ref, ssdn_ref, expand_ref, poolw_ref, pools_ref,
              wo_ref, o_ref, xext, uext, state, z_s, xs_s, bc_s, ee_s, y_s, sj):
    ts = MIX_TS

    @pl.when(sj == 0)
    def _():
        xext[0:CONV_HIST, :] = jnp.zeros((CONV_HIST, CONV_DIM), F32)
        uext[0:POOL_HIST, :] = jnp.zeros((POOL_HIST, POOL_DIM), F32)
        state[...] = jnp.zeros_like(state)

    x = x_ref[...]
    h = _rms(x, g_ref[...]).astype(BF16)
    z_s[...] = _dot(h, wz_ref[...])
    xext[CONV_HIST:CONV_HIST + ts, :] = _dot(h, wxbc_ref[...])
    uext[POOL_HIST:POOL_HIST + ts, :] = _dot(h, wu_ref[...])

    dt_raw = _dot(h, wdt_ref[...]) + dtb_ref[...]
    dt = jnp.maximum(dt_raw, 0.0) + jnp.log1p(jnp.exp(-jnp.abs(dt_raw)))
    cs = dt * (-jnp.exp(alog_ref[...]))
    row_in_chunk = lax.broadcasted_iota(jnp.int32, (ts, LANES), 0) % CHUNK
    shift = 1
    while shift < CHUNK:
        cs = cs + jnp.where(row_in_chunk >= shift, pltpu.roll(cs, shift, axis=0), 0.0)
        shift *= 2

    slot = lax.broadcasted_iota(jnp.int32, (ts, LANES), 1) // HEAD_SLOT
    terms = _split3(dt) + _split3(cs)
    packed = jnp.zeros((ts, LANES), F32)
    for k, term in enumerate(terms):
        packed = jnp.where(slot == k, term, packed)
    ee_s[...] = _dot(packed.astype(BF16), expand_ref[...])

    conv = convb_ref[...] + convw_ref[SSD_CONV - 1:SSD_CONV, :] * xext[CONV_HIST:CONV_HIST + ts, :]
    for k in range(SSD_CONV - 1):
        off = CONV_HIST - (SSD_CONV - 1) + k
        conv = conv + convw_ref[k:k + 1, :] * xext[off:off + ts, :]
    conv = _silu(conv)
    xs_s[...] = conv[:, :SSD_DIM]
    bc_s[...] = conv[:, SSD_DIM:]
    xext[0:CONV_HIST, :] = xext[ts:ts + CONV_HIST, :]

    sub = lax.broadcasted_iota(jnp.int32, (CHUNK, LANES), 0)
    lane = lax.broadcasted_iota(jnp.int32, (CHUNK, LANES), 1)
    diag = sub == lane % CHUNK
    causal = sub >= lane % CHUNK
    left = lane < CHUNK
    for c in range(ts // CHUNK):
        rows = slice(c * CHUNK, (c + 1) * CHUNK)
        for g in range(SSD_NGROUPS):
            gcols = slice(g * GROUP_DIM, (g + 1) * GROUP_DIM)
            xg = xs_s[rows, gcols]
            b_mat = bc_s[rows, g * SSD_STATE:(g + 1) * SSD_STATE]
            c_mat = bc_s[rows, BC_DIM + g * SSD_STATE:BC_DIM + (g + 1) * SSD_STATE]
            dt_e = ee_s[rows, gcols]
            cs_e = ee_s[rows, SSD_DIM + g * GROUP_DIM:SSD_DIM + (g + 1) * GROUP_DIM]
            cs_last = cs_e[CHUNK - 1:CHUNK, :]
            xdt = xg * dt_e
            xdt_b = xdt.astype(BF16)
            x_decayed = (xdt * jnp.exp(cs_last - cs_e)).astype(BF16)
            c_b = c_mat.astype(BF16)
            b_b = b_mat.astype(BF16)
            s_prev = state[g]
            y_off = _dot(c_b, s_prev.astype(BF16)) * jnp.exp(cs_e)
            s_new = _dot(b_mat.T.astype(BF16), x_decayed)
            state[g] = jnp.exp(cs_last) * s_prev + s_new
            cb2 = lax.dot_general(c_b, jnp.concatenate([b_b, b_b], axis=0),
                                  (((1,), (1,)), ((), ())), preferred_element_type=F32)
            for p in range(GROUP_DIM // LANES):
                pc = slice(p * LANES, (p + 1) * LANES)
                col = cs_e[:, pc]
                rowv = jnp.sum(jnp.where(diag, col, 0.0), axis=0, keepdims=True)
                decay = jnp.exp(jnp.where(causal, col - rowv, -jnp.inf))
                g_pair = (cb2 * decay).astype(BF16)
                xp = xdt_b[:, pc]
                zero = jnp.zeros_like(xp)
                block_diag = jnp.concatenate(
                    [jnp.where(left, xp, zero), jnp.where(left, zero, xp)], axis=0)
                oc = slice(g * GROUP_DIM + p * LANES, g * GROUP_DIM + (p + 1) * LANES)
                y_s[rows, oc] = (_dot(g_pair, block_diag) + y_off[:, pc]
                                 + dskip_ref[:, oc] * xg[:, pc])

    yg = y_s[...] * _silu(z_s[...])
    normed = []
    for g in range(SSD_NGROUPS):
        seg = yg[:, g * GROUP_DIM:(g + 1) * GROUP_DIM]
        normed.append(seg * lax.rsqrt(jnp.mean(seg * seg, axis=-1, keepdims=True) + NORM_EPS))
    y_ssd = (jnp.concatenate(normed, axis=-1) * ssdn_ref[...]).astype(BF16)

    t_glob = sj * ts + lax.broadcasted_iota(jnp.int32, (ts, 1), 0)
    pooled = []
    for gi, win in enumerate(POOL_WINDOWS):
        pcols = slice(gi * POOL_GROUP_DIM, (gi + 1) * POOL_GROUP_DIM)
        acc = uext[:, pcols]
        u0 = acc[POOL_HIST:, :]
        span = 1
        while span < win:
            acc = acc + pltpu.roll(acc, span, axis=0)
            span *= 2
        acc = acc[POOL_HIST:, :]
        inv_cnt = 1.0 / jnp.minimum(t_glob + 1, win).astype(F32)
        pooled.append(_dot((acc * inv_cnt - u0).astype(BF16), poolw_ref[gi]))
    y_pool = (jnp.concatenate(pooled, axis=-1) * pools_ref[...]).astype(BF16)
    uext[0:POOL_HIST, :] = uext[ts:ts + POOL_HIST, :]

    o_ref[...] = x + _dot(y_ssd, wo_ref[0:SSD_DIM, :]) + _dot(y_pool, wo_ref[SSD_DIM:, :])


def _expand_matrix():
    e = np.zeros((LANES, 2 * SSD_DIM), np.float32)
    for k in range(6):
        for hd in range(SSD_HEADS):
            base = (k // 3) * SSD_DIM + hd * SSD_HEAD_DIM
            e[k * HEAD_SLOT + hd, base:base + SSD_HEAD_DIM] = 1.0
    return e


def _mix(x2d, seq_len, gain, w_in, w_u, w_out, pool_w, conv_w, conv_b, dt_bias, a_log, d_skip_e,
         ssd_norm, pool_scale):
    m, d = x2d.shape
    ts = MIX_TS
    tiles_per_seq = seq_len // ts
    expand = jnp.asarray(_expand_matrix(), BF16)
    small = (conv_w, conv_b, dt_bias, a_log, d_skip_e, ssd_norm, expand, pool_scale)
    tile = pl.BlockSpec((ts, d), lambda i: (jnp.maximum(i - MIX_NW, 0), 0))
    n_u = POOL_DIM // MIX_WCOLS
    n_o = (SSD_DIM + POOL_DIM) // MIX_WCOLS
    n_p = len(POOL_WINDOWS)
    in_specs = [
        tile,
        _resident(gain.shape),
        pl.BlockSpec((d, MIX_WCOLS), lambda i: (0, jnp.minimum(i, MIX_NW - 1))),
        pl.BlockSpec((d, MIX_WCOLS), lambda i: (0, jnp.minimum(i, n_u - 1))),
        pl.BlockSpec((MIX_WCOLS, d), lambda i: (jnp.minimum(i, n_o - 1), 0)),
        pl.BlockSpec((1, POOL_GROUP_DIM, POOL_GROUP_DIM), lambda i: (jnp.minimum(i, n_p - 1), 0, 0)),
    ]
    in_specs += [_resident(w.shape) for w in small]
    return pl.pallas_call(
        functools.partial(_mix_kernel, tiles_per_seq=tiles_per_seq),
        out_shape=jax.ShapeDtypeStruct((m, d), F32),
        grid=(MIX_NW + m // ts,),
        in_specs=in_specs,
        out_specs=tile,
        scratch_shapes=[
            pltpu.VMEM((d, SSD_DIM), BF16),
            pltpu.VMEM((d, CONV_DIM), BF16),
            pltpu.VMEM((d, LANES), BF16),
            pltpu.VMEM((d, POOL_DIM), BF16),
            pltpu.VMEM((SSD_DIM + POOL_DIM, d), BF16),
            pltpu.VMEM((len(POOL_WINDOWS), POOL_GROUP_DIM, POOL_GROUP_DIM), BF16),
            pltpu.VMEM((CONV_HIST + ts, CONV_DIM), F32),
            pltpu.VMEM((POOL_HIST + ts, POOL_DIM), F32),
            pltpu.VMEM((SSD_NGROUPS, SSD_STATE, GROUP_DIM), F32),
            pltpu.VMEM((ts, SSD_DIM), F32),
            pltpu.VMEM((ts, SSD_DIM), F32),
            pltpu.VMEM((ts, 2 * BC_DIM), F32),
            pltpu.VMEM((ts, 2 * SSD_DIM), F32),
            pltpu.VMEM((ts, SSD_DIM), F32),
        ],
        compiler_params=pltpu.CompilerParams(
            dimension_semantics=("arbitrary",), vmem_limit_bytes=VMEM_LIMIT),
        name="mix",
    )(x2d, gain, w_in, w_u, w_out, pool_w, *small)


def kernel(x, ffn1_norm, ffn1_w_gate, ffn1_w_up, ffn1_w_down, mix_norm, w_in, conv_w, conv_b,
           dt_bias, a_log, d_skip, ssd_norm, pool_w, pool_scale, w_out, ffn2_norm,
           ffn2_w_gate, ffn2_w_up, ffn2_w_down, final_norm):
    b, s, d = x.shape
    assert ffn1_norm.shape[0] == 1, "one macaron block"
    o3 = SSD_DIM + CONV_DIM + SSD_HEADS
    assert MIX_NW * MIX_WCOLS >= o3 and (MIX_NW - 1) * MIX_WCOLS == o3 - SSD_HEADS
    rep = LANES // SSD_HEADS
    row = lambda v: v.reshape(1, -1)
    ones = jnp.ones((1, d), F32)
    x2d = _ffn(x.reshape(b * s, d), row(ffn1_norm[0]), ffn1_w_gate[0], ffn1_w_up[0],
               ffn1_w_down[0], ones, final_norm=False, name="ffn1")
    x2d = _mix(
        x2d, s, row(mix_norm[0]), w_in[0], w_in[0][:, o3:], w_out[0], pool_w[0],
        conv_w[0], row(conv_b[0]),
        jnp.tile(row(dt_bias[0]), (1, rep)), jnp.tile(row(a_log[0]), (1, rep)),
        row(jnp.repeat(d_skip[0], SSD_HEAD_DIM)), row(ssd_norm[0]), row(pool_scale[0]))
    out = _ffn(x2d, row(ffn2_norm[0]), ffn2_w_gate[0], ffn2_w_up[0], ffn2_w_down[0],
               row(final_norm), final_norm=True, name="ffn2")
    return out.reshape(b, s, d)
```

```python
import functools

import numpy as np
import jax
import jax.numpy as jnp
from jax import lax
from jax.experimental import pallas as pl
from jax.experimental.pallas import tpu as pltpu

D_MODEL = 1024
CHUNK = 64
SSD_DIM = 1024
SSD_HEAD_DIM = 64
SSD_HEADS = 16
SSD_NGROUPS = 2
SSD_STATE = 128
SSD_CONV = 4
POOL_DIM = 1024
POOL_WINDOWS = (2, 4, 8, 16)
POOL_GROUP_DIM = 256
CONV_DIM = SSD_DIM + 2 * SSD_NGROUPS * SSD_STATE
D_FF = 2816
NORM_EPS = 1e-6

GROUP_DIM = SSD_DIM // SSD_NGROUPS
BC_DIM = SSD_NGROUPS * SSD_STATE
LANES = 128
SUBLANES = 8
HEAD_SLOT = 16
CONV_HIST = SUBLANES
POOL_HIST = 16

FFN_TM = 1024
FFN_TF = 256
FFN_NW = D_FF // FFN_TF
MIX_TS = 512
MIX_WCOLS = 256
MIX_NW = 11
VMEM_LIMIT = 56 * 1024 * 1024

F32 = jnp.float32
BF16 = jnp.bfloat16


def _silu(v):
    return v * jax.nn.sigmoid(v)


def _rms(v, gain):
    return v * lax.rsqrt(jnp.mean(v * v, axis=-1, keepdims=True) + NORM_EPS) * gain


def _dot(a, b):
    return jnp.dot(a, b, preferred_element_type=F32)


def _ffn_kernel(x_ref, g_ref, wg_ref, wu_ref, wd_ref, fg_ref, o_ref,
                wg_s, wu_s, wd_s, act_ref, *, final_norm):
    i = pl.program_id(0)

    @pl.when(i < FFN_NW)
    def _():
        wg_s[i] = wg_ref[...].astype(BF16)
        wu_s[i] = wu_ref[...].astype(BF16)
        wd_s[pl.ds(pl.multiple_of(i * FFN_TF, FFN_TF), FFN_TF), :] = wd_ref[...].astype(BF16)

    @pl.when(i >= FFN_NW)
    def _():
        x = x_ref[...]
        h = _rms(x, g_ref[...]).astype(BF16)
        for j in range(FFN_NW):
            gate = _dot(h, wg_s[j])
            up = _dot(h, wu_s[j])
            act_ref[:, j * FFN_TF:(j + 1) * FFN_TF] = (_silu(gate) * up).astype(BF16)
        y = x + 0.5 * _dot(act_ref[...], wd_s[...])
        if final_norm:
            y = _rms(y, fg_ref[...])
        o_ref[...] = y


def _resident(shape):
    return pl.BlockSpec(shape, lambda *_: (0,) * len(shape), pipeline_mode=pl.Buffered(1))


def _ffn(x2d, gain, w_gate, w_up, w_down, final_gain, *, final_norm, name):
    m = x2d.shape[0]
    wchunk = lambda i: jnp.minimum(i, FFN_NW - 1)
    tile = pl.BlockSpec((FFN_TM, D_MODEL), lambda i: (jnp.maximum(i - FFN_NW, 0), 0))
    return pl.pallas_call(
        functools.partial(_ffn_kernel, final_norm=final_norm),
        out_shape=jax.ShapeDtypeStruct((m, D_MODEL), F32),
        grid=(FFN_NW + m // FFN_TM,),
        in_specs=[
            tile,
            _resident((1, D_MODEL)),
            pl.BlockSpec((D_MODEL, FFN_TF), lambda i: (0, wchunk(i))),
            pl.BlockSpec((D_MODEL, FFN_TF), lambda i: (0, wchunk(i))),
            pl.BlockSpec((FFN_TF, D_MODEL), lambda i: (wchunk(i), 0)),
            _resident((1, D_MODEL)),
        ],
        out_specs=tile,
        scratch_shapes=[
            pltpu.VMEM((FFN_NW, D_MODEL, FFN_TF), BF16),
            pltpu.VMEM((FFN_NW, D_MODEL, FFN_TF), BF16),
            pltpu.VMEM((D_FF, D_MODEL), BF16),
            pltpu.VMEM((FFN_TM, D_FF), BF16),
        ],
        compiler_params=pltpu.CompilerParams(
            dimension_semantics=("arbitrary",), vmem_limit_bytes=VMEM_LIMIT),
        name=name,
    )(x2d, gain, w_gate, w_up, w_down, final_gain)


def _split3(v):
    hi = v.astype(BF16).astype(F32)
    r = v - hi
    mid = r.astype(BF16).astype(F32)
    return hi, mid, r - mid


def _mix_kernel(x_ref, g_ref, win_ref, wuf_ref, wof_ref, poolf_ref, convw_ref, convb_ref,
                dtb_ref, alog_ref, dskip_ref, ssdn_ref, expand_ref, pools_ref, o_ref,
                wz_ref, wxbc_ref, wdt_ref, wu_ref, wo_ref, poolw_ref,
                xext, uext, state, z_s, xs_s, bc_s, ee_s, y_s, *, tiles_per_seq):
    i = pl.program_id(0)
    z_blocks = SSD_DIM // MIX_WCOLS
    xbc_blocks = CONV_DIM // MIX_WCOLS

    for j in range(MIX_NW):
        @pl.when(i == j)
        def _(j=j):
            blk = win_ref[...].T
            if j < z_blocks:
                wz_ref[:, j * MIX_WCOLS:(j + 1) * MIX_WCOLS] = blk.astype(BF16)
            elif j < z_blocks + xbc_blocks:
                k = j - z_blocks
                wxbc_ref[:, k * MIX_WCOLS:(k + 1) * MIX_WCOLS] = blk.astype(BF16)
            else:
                lane = lax.broadcasted_iota(jnp.int32, (D_MODEL, LANES), 1)
                rep = jnp.where(lane < SSD_HEADS, blk[:, :LANES], 0.0)
                width = SSD_HEADS
                while width < LANES:
                    rep = rep + pltpu.roll(rep, width, axis=1)
                    width *= 2
                wdt_ref[...] = rep.astype(BF16)
            if j < POOL_DIM // MIX_WCOLS:
                wu_ref[:, j * MIX_WCOLS:(j + 1) * MIX_WCOLS] = wuf_ref[...].T.astype(BF16)
            if j < (SSD_DIM + POOL_DIM) // MIX_WCOLS:
                wo_ref[j * MIX_WCOLS:(j + 1) * MIX_WCOLS, :] = wof_ref[...].astype(BF16)
            if j < len(POOL_WINDOWS):
                poolw_ref[j] = poolf_ref[0].astype(BF16)

    @pl.when(i >= MIX_NW)
    def _():
        _mix_tile(x_ref, g_ref, wz_ref, wxbc_ref, wdt_ref, wu_ref, convw_ref, convb_ref,
                  dtb_ref, alog_ref, dskip_ref, ssdn_ref, expand_ref, poolw_ref, pools_ref,
                  wo_ref, o_ref, xext, uext, state, z_s, xs_s, bc_s, ee_s, y_s,
                  (i - MIX_NW) % tiles_per_seq)


def _mix_tile(x_ref, g_ref, wz_ref, wxbc_ref, wdt_ref, wu_ref, convw_ref, convb_ref,
              dtb_ref, alog_ref, dskip_ref, ssdn_ref, expand_ref, poolw_ref, pools_ref,
              wo_ref, o_ref, xext, uext, state, z_s, xs_s, bc_s, ee_s, y_s, sj):
    ts = MIX_TS

    @pl.when(sj == 0)
    def _():
        xext[0:CONV_HIST, :] = jnp.zeros((CONV_HIST, CONV_DIM), F32)
        uext[0:POOL_HIST, :] = jnp.zeros((POOL_HIST, POOL_DIM), F32)
        state[...] = jnp.zeros_like(state)

    x = x_ref[...]
    h = _rms(x, g_ref[...]).astype(BF16)
    z_s[...] = _dot(h, wz_ref[...])
    xext[CONV_HIST:CONV_HIST + ts, :] = _dot(h, wxbc_ref[...])
    uext[POOL_HIST:POOL_HIST + ts, :] = _dot(h, wu_ref[...])

    dt_raw = _dot(h, wdt_ref[...]) + dtb_ref[...]
    dt = jnp.maximum(dt_raw, 0.0) + jnp.log1p(jnp.exp(-jnp.abs(dt_raw)))
    cs = dt * (-jnp.exp(alog_ref[...]))
    row_in_chunk = lax.broadcasted_iota(jnp.int32, (ts, LANES), 0) % CHUNK
    shift = 1
    while shift < CHUNK:
        cs = cs + jnp.where(row_in_chunk >= shift, pltpu.roll(cs, shift, axis=0), 0.0)
        shift *= 2

    slot = lax.broadcasted_iota(jnp.int32, (ts, LANES), 1) // HEAD_SLOT
    terms = _split3(dt) + _split3(cs)
    packed = jnp.zeros((ts, LANES), F32)
    for k, term in enumerate(terms):
        packed = jnp.where(slot == k, term, packed)
    ee_s[...] = _dot(packed.astype(BF16), expand_ref[...])

    conv = convb_ref[...] + convw_ref[SSD_CONV - 1:SSD_CONV, :] * xext[CONV_HIST:CONV_HIST + ts, :]
    for k in range(SSD_CONV - 1):
        off = CONV_HIST - (SSD_CONV - 1) + k
        conv = conv + convw_ref[k:k + 1, :] * xext[off:off + ts, :]
    conv = _silu(conv)
    xs_s[...] = conv[:, :SSD_DIM]
    bc_s[...] = conv[:, SSD_DIM:]
    xext[0:CONV_HIST, :] = xext[ts:ts + CONV_HIST, :]

    sub = lax.broadcasted_iota(jnp.int32, (CHUNK, LANES), 0)
    lane = lax.broadcasted_iota(jnp.int32, (CHUNK, LANES), 1)
    diag = sub == lane % CHUNK
    causal = sub >= lane % CHUNK
    left = lane < CHUNK
    for c in range(ts // CHUNK):
        rows = slice(c * CHUNK, (c + 1) * CHUNK)
        for g in range(SSD_NGROUPS):
            gcols = slice(g * GROUP_DIM, (g + 1) * GROUP_DIM)
            xg = xs_s[rows, gcols]
            b_mat = bc_s[rows, g * SSD_STATE:(g + 1) * SSD_STATE]
            c_mat = bc_s[rows, BC_DIM + g * SSD_STATE:BC_DIM + (g + 1) * SSD_STATE]
            dt_e = ee_s[rows, gcols]
            cs_e = ee_s[rows, SSD_DIM + g * GROUP_DIM:SSD_DIM + (g + 1) * GROUP_DIM]
            cs_last = cs_e[CHUNK - 1:CHUNK, :]
            xdt = xg * dt_e
            xdt_b = xdt.astype(BF16)
            x_decayed = (xdt * jnp.exp(cs_last - cs_e)).astype(BF16)
            c_b = c_mat.astype(BF16)
            b_b = b_mat.astype(BF16)
            s_prev = state[g]
            y_off = _dot(c_b, s_prev.astype(BF16)) * jnp.exp(cs_e)
            s_new = _dot(b_mat.T.astype(BF16), x_decayed)
            state[g] = jnp.exp(cs_last) * s_prev + s_new
            cb2 = lax.dot_general(c_b, jnp.concatenate([b_b, b_b], axis=0),
                                  (((1,), (1,)), ((), ())), preferred_element_type=F32)
            for p in range(GROUP_DIM // LANES):
                pc = slice(p * LANES, (p + 1) * LANES)
                col = cs_e[:, pc]
                rowv = jnp.sum(jnp.where(diag, col, 0.0), axis=0, keepdims=True)
                decay = jnp.exp(jnp.where(causal, col - rowv, -jnp.inf))
                g_pair = (cb2 * decay).astype(BF16)
                xp = xdt_b[:, pc]
                zero = jnp.zeros_like(xp)
                block_diag = jnp.concatenate(
                    [jnp.where(left, xp, zero), jnp.where(left, zero, xp)], axis=0)
                oc = slice(g * GROUP_DIM + p * LANES, g * GROUP_DIM + (p + 1) * LANES)
                y_s[rows, oc] = (_dot(g_pair, block_diag) + y_off[:, pc]
                                 + dskip_ref[:, oc] * xg[:, pc])

    yg = y_s[...] * _silu(z_s[...])
    normed = []
    for g in range(SSD_NGROUPS):
        seg = yg[:, g * GROUP_DIM:(g + 1) * GROUP_DIM]
        normed.append(seg * lax.rsqrt(jnp.mean(seg * seg, axis=-1, keepdims=True) + NORM_EPS))
    y_ssd = (jnp.concatenate(normed, axis=-1) * ssdn_ref[...]).astype(BF16)

    t_glob = sj * ts + lax.broadcasted_iota(jnp.int32, (ts, 1), 0)
    pooled = []
    for gi, win in enumerate(POOL_WINDOWS):
        pcols = slice(gi * POOL_GROUP_DIM, (gi + 1) * POOL_GROUP_DIM)
        acc = uext[:, pcols]
        u0 = acc[POOL_HIST:, :]
        span = 1
        while span < win:
            acc = acc + pltpu.roll(acc, span, axis=0)
            span *= 2
        acc = acc[POOL_HIST:, :]
        inv_cnt = 1.0 / jnp.minimum(t_glob + 1, win).astype(F32)
        pooled.append(_dot((acc * inv_cnt - u0).astype(BF16), poolw_ref[gi]))
    y_pool = (jnp.concatenate(pooled, axis=-1) * pools_ref[...]).astype(BF16)
    uext[0:POOL_HIST, :] = uext[ts:ts + POOL_HIST, :]

    o_ref[...] = x + _dot(y_ssd, wo_ref[0:SSD_DIM, :]) + _dot(y_pool, wo_ref[SSD_DIM:, :])


def _expand_matrix():
    e = np.zeros((LANES, 2 * SSD_DIM), np.float32)
    for k in range(6):
        for hd in range(SSD_HEADS):
            base = (k // 3) * SSD_DIM + hd * SSD_HEAD_DIM
            e[k * HEAD_SLOT + hd, base:base + SSD_HEAD_DIM] = 1.0
    return e


def _mix(x2d, seq_len, gain, w_in, w_u, w_out, pool_w, conv_w, conv_b, dt_bias, a_log, d_skip_e,
         ssd_norm, pool_scale):
    m, d = x2d.shape
    ts = MIX_TS
    tiles_per_seq = seq_len // ts
    expand = jnp.asarray(_expand_matrix(), BF16)
    small = (conv_w, conv_b, dt_bias, a_log, d_skip_e, ssd_norm, expand, pool_scale)
    tile = pl.BlockSpec((ts, d), lambda i: (jnp.maximum(i - MIX_NW, 0), 0))
    n_u = POOL_DIM // MIX_WCOLS
    n_o = (SSD_DIM + POOL_DIM) // MIX_WCOLS
    n_p = len(POOL_WINDOWS)
    in_specs = [
        tile,
        _resident(gain.shape),
        pl.BlockSpec((MIX_WCOLS, d), lambda i: (jnp.minimum(i, MIX_NW - 1), 0)),
        pl.BlockSpec((MIX_WCOLS, d), lambda i: (jnp.minimum(i, n_u - 1), 0)),
        pl.BlockSpec((MIX_WCOLS, d), lambda i: (jnp.minimum(i, n_o - 1), 0)),
        pl.BlockSpec((1, POOL_GROUP_DIM, POOL_GROUP_DIM), lambda i: (jnp.minimum(i, n_p - 1), 0, 0)),
    ]
    in_specs += [_resident(w.shape) for w in small]
    return pl.pallas_call(
        functools.partial(_mix_kernel, tiles_per_seq=tiles_per_seq),
        out_shape=jax.ShapeDtypeStruct((m, d), F32),
        grid=(MIX_NW + m // ts,),
        in_specs=in_specs,
        out_specs=tile,
        scratch_shapes=[
            pltpu.VMEM((d, SSD_DIM), BF16),
            pltpu.VMEM((d, CONV_DIM), BF16),
            pltpu.VMEM((d, LANES), BF16),
            pltpu.VMEM((d, POOL_DIM), BF16),
            pltpu.VMEM((SSD_DIM + POOL_DIM, d), BF16),
            pltpu.VMEM((len(POOL_WINDOWS), POOL_GROUP_DIM, POOL_GROUP_DIM), BF16),
            pltpu.VMEM((CONV_HIST + ts, CONV_DIM), F32),
            pltpu.VMEM((POOL_HIST + ts, POOL_DIM), F32),
            pltpu.VMEM((SSD_NGROUPS, SSD_STATE, GROUP_DIM), F32),
            pltpu.VMEM((ts, SSD_DIM), F32),
            pltpu.VMEM((ts, SSD_DIM), F32),
            pltpu.VMEM((ts, 2 * BC_DIM), F32),
            pltpu.VMEM((ts, 2 * SSD_DIM), F32),
            pltpu.VMEM((ts, SSD_DIM), F32),
        ],
        compiler_params=pltpu.CompilerParams(
            dimension_semantics=("arbitrary",), vmem_limit_bytes=VMEM_LIMIT),
        name="mix",
    )(x2d, gain, w_in, w_u, w_out, pool_w, *small)


def kernel(x, ffn1_norm, ffn1_w_gate, ffn1_w_up, ffn1_w_down, mix_norm, w_in, conv_w, conv_b,
           dt_bias, a_log, d_skip, ssd_norm, pool_w, pool_scale, w_out, ffn2_norm,
           ffn2_w_gate, ffn2_w_up, ffn2_w_down, final_norm):
    b, s, d = x.shape
    assert ffn1_norm.shape[0] == 1, "one macaron block"
    o3 = SSD_DIM + CONV_DIM + SSD_HEADS
    assert MIX_NW * MIX_WCOLS >= o3 and (MIX_NW - 1) * MIX_WCOLS == o3 - SSD_HEADS
    rep = LANES // SSD_HEADS
    row = lambda v: v.reshape(1, -1)
    ones = jnp.ones((1, d), F32)
    x2d = _ffn(x.reshape(b * s, d), row(ffn1_norm[0]), ffn1_w_gate[0], ffn1_w_up[0],
               ffn1_w_down[0], ones, final_norm=False, name="ffn1")
    x2d = _mix(
        x2d, s, row(mix_norm[0]), w_in[0].T, w_in[0].T[o3:], w_out[0], pool_w[0],
        conv_w[0], row(conv_b[0]),
        jnp.tile(row(dt_bias[0]), (1, rep)), jnp.tile(row(a_log[0]), (1, rep)),
        row(jnp.repeat(d_skip[0], SSD_HEAD_DIM)), row(ssd_norm[0]), row(pool_scale[0]))
    out = _ffn(x2d, row(ffn2_norm[0]), ffn2_w_gate[0], ffn2_w_up[0], ffn2_w_down[0],
               row(final_norm), final_norm=True, name="ffn2")
    return out.reshape(b, s, d)
```

```python
import functools

import numpy as np
import jax
import jax.numpy as jnp
from jax import lax
from jax.experimental import pallas as pl
from jax.experimental.pallas import tpu as pltpu

D_MODEL = 1024
CHUNK = 64
SSD_DIM = 1024
SSD_HEAD_DIM = 64
SSD_HEADS = 16
SSD_NGROUPS = 2
SSD_STATE = 128
SSD_CONV = 4
POOL_DIM = 1024
POOL_WINDOWS = (2, 4, 8, 16)
POOL_GROUP_DIM = 256
CONV_DIM = SSD_DIM + 2 * SSD_NGROUPS * SSD_STATE
D_FF = 2816
NORM_EPS = 1e-6

GROUP_DIM = SSD_DIM // SSD_NGROUPS
BC_DIM = SSD_NGROUPS * SSD_STATE
LANES = 128
SUBLANES = 8
HEAD_SLOT = 16
CONV_HIST = SUBLANES
POOL_HIST = 16

FFN_TM = 1024
FFN_TF = 256
FFN_NW = D_FF // FFN_TF
MIX_TS = 512
MIX_WCOLS = 256
MIX_NW = 11
VMEM_LIMIT = 56 * 1024 * 1024

F32 = jnp.float32
BF16 = jnp.bfloat16


def _silu(v):
    return v * jax.nn.sigmoid(v)


def _rms(v, gain):
    return v * lax.rsqrt(jnp.mean(v * v, axis=-1, keepdims=True) + NORM_EPS) * gain


def _dot(a, b):
    return jnp.dot(a, b, preferred_element_type=F32)


def _ffn_kernel(x_ref, g_ref, wg_ref, wu_ref, wd_ref, fg_ref, o_ref,
                wg_s, wu_s, wd_s, act_ref, *, final_norm):
    i = pl.program_id(0)

    @pl.when(i < FFN_NW)
    def _():
        wg_s[i] = wg_ref[...].astype(BF16)
        wu_s[i] = wu_ref[...].astype(BF16)
        wd_s[pl.ds(pl.multiple_of(i * FFN_TF, FFN_TF), FFN_TF), :] = wd_ref[...].astype(BF16)

    @pl.when(i >= FFN_NW)
    def _():
        x = x_ref[...]
        h = _rms(x, g_ref[...]).astype(BF16)
        for j in range(FFN_NW):
            gate = _dot(h, wg_s[j])
            up = _dot(h, wu_s[j])
            act_ref[:, j * FFN_TF:(j + 1) * FFN_TF] = (_silu(gate) * up).astype(BF16)
        y = x + 0.5 * _dot(act_ref[...], wd_s[...])
        if final_norm:
            y = _rms(y, fg_ref[...])
        o_ref[...] = y


def _resident(shape):
    return pl.BlockSpec(shape, lambda *_: (0,) * len(shape), pipeline_mode=pl.Buffered(1))


def _ffn(x2d, gain, w_gate, w_up, w_down, final_gain, *, final_norm, name):
    m = x2d.shape[0]
    wchunk = lambda i: jnp.minimum(i, FFN_NW - 1)
    tile = pl.BlockSpec((FFN_TM, D_MODEL), lambda i: (jnp.maximum(i - FFN_NW, 0), 0))
    return pl.pallas_call(
        functools.partial(_ffn_kernel, final_norm=final_norm),
        out_shape=jax.ShapeDtypeStruct((m, D_MODEL), F32),
        grid=(FFN_NW + m // FFN_TM,),
        in_specs=[
            tile,
            _resident((1, D_MODEL)),
            pl.BlockSpec((D_MODEL, FFN_TF), lambda i: (0, wchunk(i))),
            pl.BlockSpec((D_MODEL, FFN_TF), lambda i: (0, wchunk(i))),
            pl.BlockSpec((FFN_TF, D_MODEL), lambda i: (wchunk(i), 0)),
            _resident((1, D_MODEL)),
        ],
        out_specs=tile,
        scratch_shapes=[
            pltpu.VMEM((FFN_NW, D_MODEL, FFN_TF), BF16),
            pltpu.VMEM((FFN_NW, D_MODEL, FFN_TF), BF16),
            pltpu.VMEM((D_FF, D_MODEL), BF16),
            pltpu.VMEM((FFN_TM, D_FF), BF16),
        ],
        compiler_params=pltpu.CompilerParams(
            dimension_semantics=("arbitrary",), vmem_limit_bytes=VMEM_LIMIT),
        name=name,
    )(x2d, gain, w_gate, w_up, w_down, final_gain)


def _split3(v):
    hi = v.astype(BF16).astype(F32)
    r = v - hi
    mid = r.astype(BF16).astype(F32)
    return hi, mid, r - mid


def _mix_kernel(x_ref, g_ref, win_ref, wuf_ref, wof_ref, poolf_ref, convw_ref, convb_ref,
                dtb_ref, alog_ref, dskip_ref, ssdn_ref, expand_ref, pools_ref, o_ref,
                wz_ref, wxbc_ref, wdt_ref, wu_ref, wo_ref, poolw_ref,
                xext, uext, state, z_s, xs_s, bc_s, ee_s, y_s, *, tiles_per_seq):
    i = pl.program_id(0)
    z_blocks = SSD_DIM // MIX_WCOLS
    xbc_blocks = CONV_DIM // MIX_WCOLS

    for j in range(MIX_NW):
        @pl.when(i == j)
        def _(j=j):
            blk = win_ref[...].T
            if j < z_blocks:
                wz_ref[:, j * MIX_WCOLS:(j + 1) * MIX_WCOLS] = blk.astype(BF16)
            elif j < z_blocks + xbc_blocks:
                k = j - z_blocks
                wxbc_ref[:, k * MIX_WCOLS:(k + 1) * MIX_WCOLS] = blk.astype(BF16)
            else:
                lane = lax.broadcasted_iota(jnp.int32, (D_MODEL, LANES), 1)
                rep = jnp.where(lane < SSD_HEADS, blk[:, :LANES], 0.0)
                width = SSD_HEADS
                while width < LANES:
                    rep = rep + pltpu.roll(rep, width, axis=1)
                    width *= 2
                wdt_ref[...] = rep.astype(BF16)
            if j < POOL_DIM // MIX_WCOLS:
                wu_ref[:, j * MIX_WCOLS:(j + 1) * MIX_WCOLS] = wuf_ref[...].T.astype(BF16)
            if j < (SSD_DIM + POOL_DIM) // MIX_WCOLS:
                wo_ref[j * MIX_WCOLS:(j + 1) * MIX_WCOLS, :] = wof_ref[...].astype(BF16)
            if j < len(POOL_WINDOWS):
                poolw_ref[j] = poolf_ref[0].astype(BF16)

    @pl.when(i >= MIX_NW)
    def _():
        _mix_tile(x_ref, g_ref, wz_ref, wxbc_ref, wdt_ref, wu_ref, convw_ref, convb_ref,
                  dtb_ref, alog_ref, dskip_ref, ssdn_ref, expand_ref, poolw_ref, pools_ref,
                  wo_ref, o_ref, xext, uext, state, z_s, xs_s, bc_s, ee_s, y_s,
                  (i - MIX_NW) % tiles_per_seq)


def _mix_tile(x_ref, g_ref, wz_ref, wxbc_ref, wdt_ref, wu_ref, convw_ref, convb_ref,
              dtb_ref, alog_ref, dskip_ref, ssdn_ref, expand_ref, poolw_ref, pools_ref,
              wo_ref, o_ref, xext, uext, state, z_s, xs_s, bc_s, ee_s, y_s, sj):
    ts = MIX_TS

    @pl.when(sj == 0)
    def _():
        xext[0:CONV_HIST, :] = jnp.zeros((CONV_HIST, CONV_DIM), F32)
        uext[0:POOL_HIST, :] = jnp.zeros((POOL_HIST, POOL_DIM), F32)
        state[...] = jnp.zeros_like(state)

    x = x_ref[...]
    h = _rms(x, g_ref[...]).astype(BF16)
    xext[CONV_HIST:CONV_HIST + ts, :] = _dot(h, wxbc_ref[...])
    uext[POOL_HIST:POOL_HIST + ts, :] = _dot(h, wu_ref[...])

    dt_raw = _dot(h, wdt_ref[...]) + dtb_ref[...]
    dt = jnp.maximum(dt_raw, 0.0) + jnp.log1p(jnp.exp(-jnp.abs(dt_raw)))
    cs = dt * (-jnp.exp(alog_ref[...]))
    row_in_chunk = lax.broadcasted_iota(jnp.int32, (ts, LANES), 0) % CHUNK
    shift = 1
    while shift < CHUNK:
        cs = cs + jnp.where(row_in_chunk >= shift, pltpu.roll(cs, shift, axis=0), 0.0)
        shift *= 2

    slot = lax.broadcasted_iota(jnp.int32, (ts, LANES), 1) // HEAD_SLOT
    terms = _split3(dt) + _split3(cs)
    packed = jnp.zeros((ts, LANES), F32)
    for k, term in enumerate(terms):
        packed = jnp.where(slot == k, term, packed)
    ee_s[...] = _dot(packed.astype(BF16), expand_ref[...])

    conv = convb_ref[...] + convw_ref[SSD_CONV - 1:SSD_CONV, :] * xext[CONV_HIST:CONV_HIST + ts, :]
    for k in range(SSD_CONV - 1):
        off = CONV_HIST - (SSD_CONV - 1) + k
        conv = conv + convw_ref[k:k + 1, :] * xext[off:off + ts, :]
    conv = _silu(conv)
    xs_s[...] = conv[:, :SSD_DIM]
    bc_s[...] = conv[:, SSD_DIM:]
    xext[0:CONV_HIST, :] = xext[ts:ts + CONV_HIST, :]

    t_glob = sj * ts + lax.broadcasted_iota(jnp.int32, (ts, 1), 0)
    pooled = []

    def pool_group(gi):
        win = POOL_WINDOWS[gi]
        pcols = slice(gi * POOL_GROUP_DIM, (gi + 1) * POOL_GROUP_DIM)
        acc = uext[:, pcols]
        u0 = acc[POOL_HIST:, :]
        span = 1
        while span < win:
            acc = acc + pltpu.roll(acc, span, axis=0)
            span *= 2
        acc = acc[POOL_HIST:, :]
        inv_cnt = 1.0 / jnp.minimum(t_glob + 1, win).astype(F32)
        pooled.append(_dot((acc * inv_cnt - u0).astype(BF16), poolw_ref[gi]))

    sub = lax.broadcasted_iota(jnp.int32, (CHUNK, LANES), 0)
    lane = lax.broadcasted_iota(jnp.int32, (CHUNK, LANES), 1)
    diag = sub == lane % CHUNK
    causal = sub >= lane % CHUNK
    left = lane < CHUNK
    for c in range(ts // CHUNK):
        rows = slice(c * CHUNK, (c + 1) * CHUNK)
        for g in range(SSD_NGROUPS):
            gcols = slice(g * GROUP_DIM, (g + 1) * GROUP_DIM)
            xg = xs_s[rows, gcols]
            b_mat = bc_s[rows, g * SSD_STATE:(g + 1) * SSD_STATE]
            c_mat = bc_s[rows, BC_DIM + g * SSD_STATE:BC_DIM + (g + 1) * SSD_STATE]
            dt_e = ee_s[rows, gcols]
            cs_e = ee_s[rows, SSD_DIM + g * GROUP_DIM:SSD_DIM + (g + 1) * GROUP_DIM]
            cs_last = cs_e[CHUNK - 1:CHUNK, :]
            xdt = xg * dt_e
            xdt_b = xdt.astype(BF16)
            x_decayed = (xdt * jnp.exp(cs_last - cs_e)).astype(BF16)
            c_b = c_mat.astype(BF16)
            b_b = b_mat.astype(BF16)
            s_prev = state[g]
            y_off = _dot(c_b, s_prev.astype(BF16)) * jnp.exp(cs_e)
            s_new = _dot(b_mat.T.astype(BF16), x_decayed)
            state[g] = jnp.exp(cs_last) * s_prev + s_new
            cb2 = lax.dot_general(c_b, jnp.concatenate([b_b, b_b], axis=0),
                                  (((1,), (1,)), ((), ())), preferred_element_type=F32)
            for p in range(GROUP_DIM // LANES):
                pc = slice(p * LANES, (p + 1) * LANES)
                col = cs_e[:, pc]
                rowv = jnp.sum(jnp.where(diag, col, 0.0), axis=0, keepdims=True)
                decay = jnp.exp(jnp.where(causal, col - rowv, -jnp.inf))
                g_pair = (cb2 * decay).astype(BF16)
                xp = xdt_b[:, pc]
                zero = jnp.zeros_like(xp)
                block_diag = jnp.concatenate(
                    [jnp.where(left, xp, zero), jnp.where(left, zero, xp)], axis=0)
                oc = slice(g * GROUP_DIM + p * LANES, g * GROUP_DIM + (p + 1) * LANES)
                y_s[rows, oc] = (_dot(g_pair, block_diag) + y_off[:, pc]
                                 + dskip_ref[:, oc] * xg[:, pc])
        if c % 2 == 1:
            pool_group(c // 2)

    z_s[...] = _dot(h, wz_ref[...])
    yg = y_s[...] * _silu(z_s[...])
    normed = []
    for g in range(SSD_NGROUPS):
        seg = yg[:, g * GROUP_DIM:(g + 1) * GROUP_DIM]
        normed.append(seg * lax.rsqrt(jnp.mean(seg * seg, axis=-1, keepdims=True) + NORM_EPS))
    y_ssd = (jnp.concatenate(normed, axis=-1) * ssdn_ref[...]).astype(BF16)

    y_pool = (jnp.concatenate(pooled, axis=-1) * pools_ref[...]).astype(BF16)
    uext[0:POOL_HIST, :] = uext[ts:ts + POOL_HIST, :]

    o_ref[...] = x + _dot(jnp.concatenate([y_ssd, y_pool], axis=-1), wo_ref[...])


def _expand_matrix():
    e = np.zeros((LANES, 2 * SSD_DIM), np.float32)
    for k in range(6):
        for hd in range(SSD_HEADS):
            base = (k // 3) * SSD_DIM + hd * SSD_HEAD_DIM
            e[k * HEAD_SLOT + hd, base:base + SSD_HEAD_DIM] = 1.0
    return e


def _mix(x2d, seq_len, gain, w_in, w_u, w_out, pool_w, conv_w, conv_b, dt_bias, a_log, d_skip_e,
         ssd_norm, pool_scale):
    m, d = x2d.shape
    ts = MIX_TS
    tiles_per_seq = seq_len // ts
    expand = jnp.asarray(_expand_matrix(), BF16)
    small = (conv_w, conv_b, dt_bias, a_log, d_skip_e, ssd_norm, expand, pool_scale)
    tile = pl.BlockSpec((ts, d), lambda i: (jnp.maximum(i - MIX_NW, 0), 0))
    n_u = POOL_DIM // MIX_WCOLS
    n_o = (SSD_DIM + POOL_DIM) // MIX_WCOLS
    n_p = len(POOL_WINDOWS)
    in_specs = [
        tile,
        _resident(gain.shape),
        pl.BlockSpec((MIX_WCOLS, d), lambda i: (jnp.minimum(i, MIX_NW - 1), 0)),
        pl.BlockSpec((MIX_WCOLS, d), lambda i: (jnp.minimum(i, n_u - 1), 0)),
        pl.BlockSpec((MIX_WCOLS, d), lambda i: (jnp.minimum(i, n_o - 1), 0)),
        pl.BlockSpec((1, POOL_GROUP_DIM, POOL_GROUP_DIM), lambda i: (jnp.minimum(i, n_p - 1), 0, 0)),
    ]
    in_specs += [_resident(w.shape) for w in small]
    return pl.pallas_call(
        functools.partial(_mix_kernel, tiles_per_seq=tiles_per_seq),
        out_shape=jax.ShapeDtypeStruct((m, d), F32),
        grid=(MIX_NW + m // ts,),
        in_specs=in_specs,
        out_specs=tile,
        scratch_shapes=[
            pltpu.VMEM((d, SSD_DIM), BF16),
            pltpu.VMEM((d, CONV_DIM), BF16),
            pltpu.VMEM((d, LANES), BF16),
            pltpu.VMEM((d, POOL_DIM), BF16),
            pltpu.VMEM((SSD_DIM + POOL_DIM, d), BF16),
            pltpu.VMEM((len(POOL_WINDOWS), POOL_GROUP_DIM, POOL_GROUP_DIM), BF16),
            pltpu.VMEM((CONV_HIST + ts, CONV_DIM), F32),
            pltpu.VMEM((POOL_HIST + ts, POOL_DIM), F32),
            pltpu.VMEM((SSD_NGROUPS, SSD_STATE, GROUP_DIM), F32),
            pltpu.VMEM((ts, SSD_DIM), F32),
            pltpu.VMEM((ts, SSD_DIM), F32),
            pltpu.VMEM((ts, 2 * BC_DIM), F32),
            pltpu.VMEM((ts, 2 * SSD_DIM), F32),
            pltpu.VMEM((ts, SSD_DIM), F32),
        ],
        compiler_params=pltpu.CompilerParams(
            dimension_semantics=("arbitrary",), vmem_limit_bytes=VMEM_LIMIT),
        name="mix",
    )(x2d, gain, w_in, w_u, w_out, pool_w, *small)


def kernel(x, ffn1_norm, ffn1_w_gate, ffn1_w_up, ffn1_w_down, mix_norm, w_in, conv_w, conv_b,
           dt_bias, a_log, d_skip, ssd_norm, pool_w, pool_scale, w_out, ffn2_norm,
           ffn2_w_gate, ffn2_w_up, ffn2_w_down, final_norm):
    b, s, d = x.shape
    assert ffn1_norm.shape[0] == 1, "one macaron block"
    o3 = SSD_DIM + CONV_DIM + SSD_HEADS
    assert MIX_NW * MIX_WCOLS >= o3 and (MIX_NW - 1) * MIX_WCOLS == o3 - SSD_HEADS
    rep = LANES // SSD_HEADS
    row = lambda v: v.reshape(1, -1)
    ones = jnp.ones((1, d), F32)
    x2d = _ffn(x.reshape(b * s, d), row(ffn1_norm[0]), ffn1_w_gate[0], ffn1_w_up[0],
               ffn1_w_down[0], ones, final_norm=False, name="ffn1")
    x2d = _mix(
        x2d, s, row(mix_norm[0]), w_in[0].T, w_in[0].T[o3:], w_out[0], pool_w[0],
        conv_w[0], row(conv_b[0]),
        jnp.tile(row(dt_bias[0]), (1, rep)), jnp.tile(row(a_log[0]), (1, rep)),
        row(jnp.repeat(d_skip[0], SSD_HEAD_DIM)), row(ssd_norm[0]), row(pool_scale[0]))
    out = _ffn(x2d, row(ffn2_norm[0]), ffn2_w_gate[0], ffn2_w_up[0], ffn2_w_down[0],
               row(final_norm), final_norm=True, name="ffn2")
    return out.reshape(b, s, d)
```

```python
import functools

import numpy as np
import jax
import jax.numpy as jnp
from jax import lax
from jax.experimental import pallas as pl
from jax.experimental.pallas import tpu as pltpu

D_MODEL = 1024
CHUNK = 64
SSD_DIM = 1024
SSD_HEAD_DIM = 64
SSD_HEADS = 16
SSD_NGROUPS = 2
SSD_STATE = 128
SSD_CONV = 4
POOL_DIM = 1024
POOL_WINDOWS = (2, 4, 8, 16)
POOL_GROUP_DIM = 256
CONV_DIM = SSD_DIM + 2 * SSD_NGROUPS * SSD_STATE
D_FF = 2816
NORM_EPS = 1e-6

GROUP_DIM = SSD_DIM // SSD_NGROUPS
BC_DIM = SSD_NGROUPS * SSD_STATE
LANES = 128
SUBLANES = 8
HEAD_SLOT = 16
CONV_HIST = SUBLANES
POOL_HIST = 16

FFN_TM = 1024
FFN_TF = 256
FFN_NW = D_FF // FFN_TF
MIX_TS = 512
MIX_WCOLS = 256
MIX_NW = 11
VMEM_LIMIT = 56 * 1024 * 1024

F32 = jnp.float32
BF16 = jnp.bfloat16


def _silu(v):
    return v * jax.nn.sigmoid(v)


def _rms(v, gain):
    return v * lax.rsqrt(jnp.mean(v * v, axis=-1, keepdims=True) + NORM_EPS) * gain


def _dot(a, b):
    return jnp.dot(a, b, preferred_element_type=F32)


def _ffn_kernel(x_ref, g_ref, wg_ref, wu_ref, wd_ref, fg_ref, o_ref,
                wg_s, wu_s, wd_s, act_ref, h_s, *, final_norm):
    i = pl.program_id(0)

    def gate_up(h, j):
        gate = _dot(h, wg_s[j])
        up = _dot(h, wu_s[j])
        act_ref[:, j * FFN_TF:(j + 1) * FFN_TF] = (_silu(gate) * up).astype(BF16)

    def finish(x):
        y = x + 0.5 * _dot(act_ref[...], wd_s[...])
        if final_norm:
            y = _rms(y, fg_ref[...])
        o_ref[...] = y

    for j in range(FFN_NW):
        @pl.when(i == j)
        def _(j=j):
            wg_s[j] = wg_ref[...].astype(BF16)
            wu_s[j] = wu_ref[...].astype(BF16)
            wd_s[j * FFN_TF:(j + 1) * FFN_TF, :] = wd_ref[...].astype(BF16)
            if j == 0:
                h_s[...] = _rms(x_ref[...], g_ref[...]).astype(BF16)
            gate_up(h_s[...], j)

    @pl.when(i == FFN_NW)
    def _():
        finish(x_ref[...])

    @pl.when(i > FFN_NW)
    def _():
        x = x_ref[...]
        h = _rms(x, g_ref[...]).astype(BF16)
        for j in range(FFN_NW):
            gate_up(h, j)
        finish(x)


def _resident(shape):
    return pl.BlockSpec(shape, lambda *_: (0,) * len(shape), pipeline_mode=pl.Buffered(1))


def _ffn(x2d, gain, w_gate, w_up, w_down, final_gain, *, final_norm, name):
    m = x2d.shape[0]
    wchunk = lambda i: jnp.minimum(i, FFN_NW - 1)
    tile = pl.BlockSpec((FFN_TM, D_MODEL), lambda i: (jnp.maximum(i - FFN_NW, 0), 0))
    return pl.pallas_call(
        functools.partial(_ffn_kernel, final_norm=final_norm),
        out_shape=jax.ShapeDtypeStruct((m, D_MODEL), F32),
        grid=(FFN_NW + m // FFN_TM,),
        in_specs=[
            tile,
            _resident((1, D_MODEL)),
            pl.BlockSpec((D_MODEL, FFN_TF), lambda i: (0, wchunk(i))),
            pl.BlockSpec((D_MODEL, FFN_TF), lambda i: (0, wchunk(i))),
            pl.BlockSpec((FFN_TF, D_MODEL), lambda i: (wchunk(i), 0)),
            _resident((1, D_MODEL)),
        ],
        out_specs=tile,
        scratch_shapes=[
            pltpu.VMEM((FFN_NW, D_MODEL, FFN_TF), BF16),
            pltpu.VMEM((FFN_NW, D_MODEL, FFN_TF), BF16),
            pltpu.VMEM((D_FF, D_MODEL), BF16),
            pltpu.VMEM((FFN_TM, D_FF), BF16),
            pltpu.VMEM((FFN_TM, D_MODEL), BF16),
        ],
        compiler_params=pltpu.CompilerParams(
            dimension_semantics=("arbitrary",), vmem_limit_bytes=VMEM_LIMIT),
        name=name,
    )(x2d, gain, w_gate, w_up, w_down, final_gain)


def _split3(v):
    hi = v.astype(BF16).astype(F32)
    r = v - hi
    mid = r.astype(BF16).astype(F32)
    return hi, mid, r - mid


def _mix_kernel(x_ref, g_ref, win_ref, wuf_ref, wof_ref, poolf_ref, convw_ref, convb_ref,
                dtb_ref, alog_ref, dskip_ref, ssdn_ref, expand_ref, pools_ref, o_ref,
                wz_ref, wxbc_ref, wdt_ref, wu_ref, wo_ref, poolw_ref,
                xext, uext, state, z_s, xs_s, bc_s, ee_s, y_s, *, tiles_per_seq):
    i = pl.program_id(0)
    z_blocks = SSD_DIM // MIX_WCOLS
    xbc_blocks = CONV_DIM // MIX_WCOLS

    for j in range(MIX_NW):
        @pl.when(i == j)
        def _(j=j):
            blk = win_ref[...].T
            if j < z_blocks:
                wz_ref[:, j * MIX_WCOLS:(j + 1) * MIX_WCOLS] = blk.astype(BF16)
            elif j < z_blocks + xbc_blocks:
                k = j - z_blocks
                wxbc_ref[:, k * MIX_WCOLS:(k + 1) * MIX_WCOLS] = blk.astype(BF16)
            else:
                lane = lax.broadcasted_iota(jnp.int32, (D_MODEL, LANES), 1)
                rep = jnp.where(lane < SSD_HEADS, blk[:, :LANES], 0.0)
                width = SSD_HEADS
                while width < LANES:
                    rep = rep + pltpu.roll(rep, width, axis=1)
                    width *= 2
                wdt_ref[...] = rep.astype(BF16)
            if j < POOL_DIM // MIX_WCOLS:
                wu_ref[:, j * MIX_WCOLS:(j + 1) * MIX_WCOLS] = wuf_ref[...].T.astype(BF16)
            if j < (SSD_DIM + POOL_DIM) // MIX_WCOLS:
                wo_ref[j * MIX_WCOLS:(j + 1) * MIX_WCOLS, :] = wof_ref[...].astype(BF16)
            if j < len(POOL_WINDOWS):
                poolw_ref[j] = poolf_ref[0].astype(BF16)

    @pl.when(i >= MIX_NW)
    def _():
        _mix_tile(x_ref, g_ref, wz_ref, wxbc_ref, wdt_ref, wu_ref, convw_ref, convb_ref,
                  dtb_ref, alog_ref, dskip_ref, ssdn_ref, expand_ref, poolw_ref, pools_ref,
                  wo_ref, o_ref, xext, uext, state, z_s, xs_s, bc_s, ee_s, y_s,
                  (i - MIX_NW) % tiles_per_seq)


def _mix_tile(x_ref, g_ref, wz_ref, wxbc_ref, wdt_ref, wu_ref, convw_ref, convb_ref,
              dtb_ref, alog_ref, dskip_ref, ssdn_ref, expand_ref, poolw_ref, pools_ref,
              wo_ref, o_ref, xext, uext, state, z_s, xs_s, bc_s, ee_s, y_s, sj):
    ts = MIX_TS

    @pl.when(sj == 0)
    def _():
        xext[0:CONV_HIST, :] = jnp.zeros((CONV_HIST, CONV_DIM), F32)
        uext[0:POOL_HIST, :] = jnp.zeros((POOL_HIST, POOL_DIM), F32)
        state[...] = jnp.zeros_like(state)

    x = x_ref[...]
    h = _rms(x, g_ref[...]).astype(BF16)
    xext[CONV_HIST:CONV_HIST + ts, :] = _dot(h, wxbc_ref[...])
    uext[POOL_HIST:POOL_HIST + ts, :] = _dot(h, wu_ref[...])

    dt_raw = _dot(h, wdt_ref[...]) + dtb_ref[...]
    dt = jnp.maximum(dt_raw, 0.0) + jnp.log1p(jnp.exp(-jnp.abs(dt_raw)))
    cs = dt * (-jnp.exp(alog_ref[...]))
    row_in_chunk = lax.broadcasted_iota(jnp.int32, (ts, LANES), 0) % CHUNK
    shift = 1
    while shift < CHUNK:
        cs = cs + jnp.where(row_in_chunk >= shift, pltpu.roll(cs, shift, axis=0), 0.0)
        shift *= 2

    slot = lax.broadcasted_iota(jnp.int32, (ts, LANES), 1) // HEAD_SLOT
    terms = _split3(dt) + _split3(cs)
    packed = jnp.zeros((ts, LANES), F32)
    for k, term in enumerate(terms):
        packed = jnp.where(slot == k, term, packed)
    ee_s[...] = _dot(packed.astype(BF16), expand_ref[...])

    conv = convb_ref[...] + convw_ref[SSD_CONV - 1:SSD_CONV, :] * xext[CONV_HIST:CONV_HIST + ts, :]
    for k in range(SSD_CONV - 1):
        off = CONV_HIST - (SSD_CONV - 1) + k
        conv = conv + convw_ref[k:k + 1, :] * xext[off:off + ts, :]
    conv = _silu(conv)
    xs_s[...] = conv[:, :SSD_DIM]
    bc_s[...] = conv[:, SSD_DIM:]
    xext[0:CONV_HIST, :] = xext[ts:ts + CONV_HIST, :]

    t_glob = sj * ts + lax.broadcasted_iota(jnp.int32, (ts, 1), 0)
    pooled = []

    def pool_group(gi):
        win = POOL_WINDOWS[gi]
        pcols = slice(gi * POOL_GROUP_DIM, (gi + 1) * POOL_GROUP_DIM)
        acc = uext[:, pcols]
        u0 = acc[POOL_HIST:, :]
        span = 1
        while span < win:
            acc = acc + pltpu.roll(acc, span, axis=0)
            span *= 2
        acc = acc[POOL_HIST:, :]
        inv_cnt = 1.0 / jnp.minimum(t_glob + 1, win).astype(F32)
        pooled.append(_dot((acc * inv_cnt - u0).astype(BF16), poolw_ref[gi]))

    sub = lax.broadcasted_iota(jnp.int32, (CHUNK, LANES), 0)
    lane = lax.broadcasted_iota(jnp.int32, (CHUNK, LANES), 1)
    diag = sub == lane % CHUNK
    causal = sub >= lane % CHUNK
    left = lane < CHUNK
    for c in range(ts // CHUNK):
        rows = slice(c * CHUNK, (c + 1) * CHUNK)
        for g in range(SSD_NGROUPS):
            gcols = slice(g * GROUP_DIM, (g + 1) * GROUP_DIM)
            xg = xs_s[rows, gcols]
            b_mat = bc_s[rows, g * SSD_STATE:(g + 1) * SSD_STATE]
            c_mat = bc_s[rows, BC_DIM + g * SSD_STATE:BC_DIM + (g + 1) * SSD_STATE]
            dt_e = ee_s[rows, gcols]
            cs_e = ee_s[rows, SSD_DIM + g * GROUP_DIM:SSD_DIM + (g + 1) * GROUP_DIM]
            cs_last = cs_e[CHUNK - 1:CHUNK, :]
            xdt = xg * dt_e
            xdt_b = xdt.astype(BF16)
            x_decayed = (xdt * jnp.exp(cs_last - cs_e)).astype(BF16)
            c_b = c_mat.astype(BF16)
            b_b = b_mat.astype(BF16)
            s_prev = state[g]
            y_off = _dot(c_b, s_prev.astype(BF16)) * jnp.exp(cs_e)
            s_new = _dot(b_mat.T.astype(BF16), x_decayed)
            state[g] = jnp.exp(cs_last) * s_prev + s_new
            cb2 = lax.dot_general(c_b, jnp.concatenate([b_b, b_b], axis=0),
                                  (((1,), (1,)), ((), ())), preferred_element_type=F32)
            for p in range(GROUP_DIM // LANES):
                pc = slice(p * LANES, (p + 1) * LANES)
                col = cs_e[:, pc]
                rowv = jnp.sum(jnp.where(diag, col, 0.0), axis=0, keepdims=True)
                decay = jnp.exp(jnp.where(causal, col - rowv, -jnp.inf))
                g_pair = (cb2 * decay).astype(BF16)
                xp = xdt_b[:, pc]
                zero = jnp.zeros_like(xp)
                block_diag = jnp.concatenate(
                    [jnp.where(left, xp, zero), jnp.where(left, zero, xp)], axis=0)
                oc = slice(g * GROUP_DIM + p * LANES, g * GROUP_DIM + (p + 1) * LANES)
                y_s[rows, oc] = (_dot(g_pair, block_diag) + y_off[:, pc]
                                 + dskip_ref[:, oc] * xg[:, pc])
        if c % 2 == 1:
            pool_group(c // 2)

    z_s[...] = _dot(h, wz_ref[...])
    yg = y_s[...] * _silu(z_s[...])
    normed = []
    for g in range(SSD_NGROUPS):
        seg = yg[:, g * GROUP_DIM:(g + 1) * GROUP_DIM]
        normed.append(seg * lax.rsqrt(jnp.mean(seg * seg, axis=-1, keepdims=True) + NORM_EPS))
    y_ssd = (jnp.concatenate(normed, axis=-1) * ssdn_ref[...]).astype(BF16)

    y_pool = (jnp.concatenate(pooled, axis=-1) * pools_ref[...]).astype(BF16)
    uext[0:POOL_HIST, :] = uext[ts:ts + POOL_HIST, :]

    o_ref[...] = x + _dot(jnp.concatenate([y_ssd, y_pool], axis=-1), wo_ref[...])


def _expand_matrix():
    e = np.zeros((LANES, 2 * SSD_DIM), np.float32)
    for k in range(6):
        for hd in range(SSD_HEADS):
            base = (k // 3) * SSD_DIM + hd * SSD_HEAD_DIM
            e[k * HEAD_SLOT + hd, base:base + SSD_HEAD_DIM] = 1.0
    return e


def _mix(x2d, seq_len, gain, w_in, w_u, w_out, pool_w, conv_w, conv_b, dt_bias, a_log, d_skip_e,
         ssd_norm, pool_scale):
    m, d = x2d.shape
    ts = MIX_TS
    tiles_per_seq = seq_len // ts
    expand = jnp.asarray(_expand_matrix(), BF16)
    small = (conv_w, conv_b, dt_bias, a_log, d_skip_e, ssd_norm, expand, pool_scale)
    tile = pl.BlockSpec((ts, d), lambda i: (jnp.maximum(i - MIX_NW, 0), 0))
    n_u = POOL_DIM // MIX_WCOLS
    n_o = (SSD_DIM + POOL_DIM) // MIX_WCOLS
    n_p = len(POOL_WINDOWS)
    in_specs = [
        tile,
        _resident(gain.shape),
        pl.BlockSpec((MIX_WCOLS, d), lambda i: (jnp.minimum(i, MIX_NW - 1), 0)),
        pl.BlockSpec((MIX_WCOLS, d), lambda i: (jnp.minimum(i, n_u - 1), 0)),
        pl.BlockSpec((MIX_WCOLS, d), lambda i: (jnp.minimum(i, n_o - 1), 0)),
        pl.BlockSpec((1, POOL_GROUP_DIM, POOL_GROUP_DIM), lambda i: (jnp.minimum(i, n_p - 1), 0, 0)),
    ]
    in_specs += [_resident(w.shape) for w in small]
    return pl.pallas_call(
        functools.partial(_mix_kernel, tiles_per_seq=tiles_per_seq),
        out_shape=jax.ShapeDtypeStruct((m, d), F32),
        grid=(MIX_NW + m // ts,),
        in_specs=in_specs,
        out_specs=tile,
        scratch_shapes=[
            pltpu.VMEM((d, SSD_DIM), BF16),
            pltpu.VMEM((d, CONV_DIM), BF16),
            pltpu.VMEM((d, LANES), BF16),
            pltpu.VMEM((d, POOL_DIM), BF16),
            pltpu.VMEM((SSD_DIM + POOL_DIM, d), BF16),
            pltpu.VMEM((len(POOL_WINDOWS), POOL_GROUP_DIM, POOL_GROUP_DIM), BF16),
            pltpu.VMEM((CONV_HIST + ts, CONV_DIM), F32),
            pltpu.VMEM((POOL_HIST + ts, POOL_DIM), F32),
            pltpu.VMEM((SSD_NGROUPS, SSD_STATE, GROUP_DIM), F32),
            pltpu.VMEM((ts, SSD_DIM), F32),
            pltpu.VMEM((ts, SSD_DIM), F32),
            pltpu.VMEM((ts, 2 * BC_DIM), F32),
            pltpu.VMEM((ts, 2 * SSD_DIM), F32),
            pltpu.VMEM((ts, SSD_DIM), F32),
        ],
        compiler_params=pltpu.CompilerParams(
            dimension_semantics=("arbitrary",), vmem_limit_bytes=VMEM_LIMIT),
        name="mix",
    )(x2d, gain, w_in, w_u, w_out, pool_w, *small)


def kernel(x, ffn1_norm, ffn1_w_gate, ffn1_w_up, ffn1_w_down, mix_norm, w_in, conv_w, conv_b,
           dt_bias, a_log, d_skip, ssd_norm, pool_w, pool_scale, w_out, ffn2_norm,
           ffn2_w_gate, ffn2_w_up, ffn2_w_down, final_norm):
    b, s, d = x.shape
    assert ffn1_norm.shape[0] == 1, "one macaron block"
    o3 = SSD_DIM + CONV_DIM + SSD_HEADS
    assert MIX_NW * MIX_WCOLS >= o3 and (MIX_NW - 1) * MIX_WCOLS == o3 - SSD_HEADS
    rep = LANES // SSD_HEADS
    row = lambda v: v.reshape(1, -1)
    ones = jnp.ones((1, d), F32)
    x2d = _ffn(x.reshape(b * s, d), row(ffn1_norm[0]), ffn1_w_gate[0], ffn1_w_up[0],
               ffn1_w_down[0], ones, final_norm=False, name="ffn1")
    x2d = _mix(
        x2d, s, row(mix_norm[0]), w_in[0].T, w_in[0].T[o3:], w_out[0], pool_w[0],
        conv_w[0], row(conv_b[0]),
        jnp.tile(row(dt_bias[0]), (1, rep)), jnp.tile(row(a_log[0]), (1, rep)),
        row(jnp.repeat(d_skip[0], SSD_HEAD_DIM)), row(ssd_norm[0]), row(pool_scale[0]))
    out = _ffn(x2d, row(ffn2_norm[0]), ffn2_w_gate[0], ffn2_w_up[0], ffn2_w_down[0],
               row(final_norm), final_norm=True, name="ffn2")
    return out.reshape(b, s, d)
```

```python
import functools

import numpy as np
import jax
import jax.numpy as jnp
from jax import lax
from jax.experimental import pallas as pl
from jax.experimental.pallas import tpu as pltpu

D_MODEL = 1024
CHUNK = 64
SSD_DIM = 1024
SSD_HEAD_DIM = 64
SSD_HEADS = 16
SSD_NGROUPS = 2
SSD_STATE = 128
SSD_CONV = 4
POOL_DIM = 1024
POOL_WINDOWS = (2, 4, 8, 16)
POOL_GROUP_DIM = 256
CONV_DIM = SSD_DIM + 2 * SSD_NGROUPS * SSD_STATE
D_FF = 2816
NORM_EPS = 1e-6

GROUP_DIM = SSD_DIM // SSD_NGROUPS
BC_DIM = SSD_NGROUPS * SSD_STATE
LANES = 128
SUBLANES = 8
HEAD_SLOT = 16
CONV_HIST = SUBLANES
POOL_HIST = 16

FFN_TM = 1024
FFN_TF = 256
FFN_NW = D_FF // FFN_TF
MIX_TS = 512
MIX_WCOLS = 256
MIX_NW = 11
SCAN_NW = (SSD_DIM + POOL_DIM) // MIX_WCOLS
PROJ_TM = 512
VMEM_LIMIT = 56 * 1024 * 1024

F32 = jnp.float32
BF16 = jnp.bfloat16


def _silu(v):
    return v * jax.nn.sigmoid(v)


def _rms(v, gain):
    return v * lax.rsqrt(jnp.mean(v * v, axis=-1, keepdims=True) + NORM_EPS) * gain


def _dot(a, b):
    return jnp.dot(a, b, preferred_element_type=F32)


def _ffn_kernel(x_ref, g_ref, wg_ref, wu_ref, wd_ref, fg_ref, o_ref,
                wg_s, wu_s, wd_s, act_ref, *, final_norm):
    i = pl.program_id(0)

    @pl.when(i < FFN_NW)
    def _():
        wg_s[i] = wg_ref[...].astype(BF16)
        wu_s[i] = wu_ref[...].astype(BF16)
        wd_s[pl.ds(pl.multiple_of(i * FFN_TF, FFN_TF), FFN_TF), :] = wd_ref[...].astype(BF16)

    @pl.when(i >= FFN_NW)
    def _():
        x = x_ref[...]
        h = _rms(x, g_ref[...]).astype(BF16)
        for j in range(FFN_NW):
            gate = _dot(h, wg_s[j])
            up = _dot(h, wu_s[j])
            act_ref[:, j * FFN_TF:(j + 1) * FFN_TF] = (_silu(gate) * up).astype(BF16)
        y = x + 0.5 * _dot(act_ref[...], wd_s[...])
        if final_norm:
            y = _rms(y, fg_ref[...])
        o_ref[...] = y


def _resident(shape):
    return pl.BlockSpec(shape, lambda *_: (0,) * len(shape), pipeline_mode=pl.Buffered(1))


def _ffn(x2d, gain, w_gate, w_up, w_down, final_gain, *, final_norm, name):
    m = x2d.shape[0]
    wchunk = lambda i: jnp.minimum(i, FFN_NW - 1)
    tile = pl.BlockSpec((FFN_TM, D_MODEL), lambda i: (jnp.maximum(i - FFN_NW, 0), 0))
    return pl.pallas_call(
        functools.partial(_ffn_kernel, final_norm=final_norm),
        out_shape=jax.ShapeDtypeStruct((m, D_MODEL), F32),
        grid=(FFN_NW + m // FFN_TM,),
        in_specs=[
            tile,
            _resident((1, D_MODEL)),
            pl.BlockSpec((D_MODEL, FFN_TF), lambda i: (0, wchunk(i))),
            pl.BlockSpec((D_MODEL, FFN_TF), lambda i: (0, wchunk(i))),
            pl.BlockSpec((FFN_TF, D_MODEL), lambda i: (wchunk(i), 0)),
            _resident((1, D_MODEL)),
        ],
        out_specs=tile,
        scratch_shapes=[
            pltpu.VMEM((FFN_NW, D_MODEL, FFN_TF), BF16),
            pltpu.VMEM((FFN_NW, D_MODEL, FFN_TF), BF16),
            pltpu.VMEM((D_FF, D_MODEL), BF16),
            pltpu.VMEM((FFN_TM, D_FF), BF16),
        ],
        compiler_params=pltpu.CompilerParams(
            dimension_semantics=("arbitrary",), vmem_limit_bytes=VMEM_LIMIT),
        name=name,
    )(x2d, gain, w_gate, w_up, w_down, final_gain)


def _split3(v):
    hi = v.astype(BF16).astype(F32)
    r = v - hi
    mid = r.astype(BF16).astype(F32)
    return hi, mid, r - mid


def _proj_kernel(x_ref, g_ref, win_ref, wuf_ref, z_ref, xbc_ref, dt_ref, u_ref,
                 wz_s, wxbc_s, wdt_s, wu_s):
    i = pl.program_id(0)
    z_blocks = SSD_DIM // MIX_WCOLS
    xbc_blocks = CONV_DIM // MIX_WCOLS

    for j in range(MIX_NW):
        @pl.when(i == j)
        def _(j=j):
            blk = win_ref[...].T
            if j < z_blocks:
                wz_s[:, j * MIX_WCOLS:(j + 1) * MIX_WCOLS] = blk.astype(BF16)
            elif j < z_blocks + xbc_blocks:
                k = j - z_blocks
                wxbc_s[:, k * MIX_WCOLS:(k + 1) * MIX_WCOLS] = blk.astype(BF16)
            else:
                lane = lax.broadcasted_iota(jnp.int32, (D_MODEL, LANES), 1)
                rep = jnp.where(lane < SSD_HEADS, blk[:, :LANES], 0.0)
                width = SSD_HEADS
                while width < LANES:
                    rep = rep + pltpu.roll(rep, width, axis=1)
                    width *= 2
                wdt_s[...] = rep.astype(BF16)
            if j < POOL_DIM // MIX_WCOLS:
                wu_s[:, j * MIX_WCOLS:(j + 1) * MIX_WCOLS] = wuf_ref[...].T.astype(BF16)

    @pl.when(i >= MIX_NW)
    def _():
        h = _rms(x_ref[...], g_ref[...]).astype(BF16)
        xbc_ref[...] = _dot(h, wxbc_s[...])
        u_ref[...] = _dot(h, wu_s[...])
        z_ref[...] = _dot(h, wz_s[...])
        dt_ref[...] = _dot(h, wdt_s[...])


def _proj(x2d, gain, w_in, w_u):
    m, d = x2d.shape
    tile = lambda n: pl.BlockSpec((PROJ_TM, n), lambda i: (jnp.maximum(i - MIX_NW, 0), 0))
    n_u = POOL_DIM // MIX_WCOLS
    return pl.pallas_call(
        _proj_kernel,
        out_shape=(jax.ShapeDtypeStruct((m, SSD_DIM), F32), jax.ShapeDtypeStruct((m, CONV_DIM), F32),
                   jax.ShapeDtypeStruct((m, LANES), F32), jax.ShapeDtypeStruct((m, POOL_DIM), F32)),
        grid=(MIX_NW + m // PROJ_TM,),
        in_specs=[
            tile(d),
            _resident(gain.shape),
            pl.BlockSpec((MIX_WCOLS, d), lambda i: (jnp.minimum(i, MIX_NW - 1), 0)),
            pl.BlockSpec((MIX_WCOLS, d), lambda i: (jnp.minimum(i, n_u - 1), 0)),
        ],
        out_specs=(tile(SSD_DIM), tile(CONV_DIM), tile(LANES), tile(POOL_DIM)),
        scratch_shapes=[
            pltpu.VMEM((d, SSD_DIM), BF16),
            pltpu.VMEM((d, CONV_DIM), BF16),
            pltpu.VMEM((d, LANES), BF16),
            pltpu.VMEM((d, POOL_DIM), BF16),
        ],
        compiler_params=pltpu.CompilerParams(
            dimension_semantics=("arbitrary",), vmem_limit_bytes=VMEM_LIMIT),
        name="proj",
    )(x2d, gain, w_in, w_u)


def _scan_kernel(x_ref, z_ref, xbc_ref, dtr_ref, u_ref, wof_ref, poolf_ref, convw_ref, convb_ref,
                 dtb_ref, alog_ref, dskip_ref, ssdn_ref, expand_ref, pools_ref, o_ref,
                 wo_ref, poolw_ref, xext, uext, state, xs_s, bc_s, ee_s, y_s, *, tiles_per_seq):
    i = pl.program_id(0)

    for j in range(SCAN_NW):
        @pl.when(i == j)
        def _(j=j):
            wo_ref[j * MIX_WCOLS:(j + 1) * MIX_WCOLS, :] = wof_ref[...].astype(BF16)
            if j < len(POOL_WINDOWS):
                poolw_ref[j] = poolf_ref[0].astype(BF16)

    @pl.when(i >= SCAN_NW)
    def _():
        _scan_tile(x_ref, z_ref, xbc_ref, dtr_ref, u_ref, convw_ref, convb_ref, dtb_ref, alog_ref,
                   dskip_ref, ssdn_ref, expand_ref, poolw_ref, pools_ref, wo_ref, o_ref,
                   xext, uext, state, xs_s, bc_s, ee_s, y_s, (i - SCAN_NW) % tiles_per_seq)


def _scan_tile(x_ref, z_ref, xbc_ref, dtr_ref, u_ref, convw_ref, convb_ref, dtb_ref, alog_ref,
               dskip_ref, ssdn_ref, expand_ref, poolw_ref, pools_ref, wo_ref, o_ref,
               xext, uext, state, xs_s, bc_s, ee_s, y_s, sj):
    ts = MIX_TS

    @pl.when(sj == 0)
    def _():
        xext[0:CONV_HIST, :] = jnp.zeros((CONV_HIST, CONV_DIM), F32)
        uext[0:POOL_HIST, :] = jnp.zeros((POOL_HIST, POOL_DIM), F32)
        state[...] = jnp.zeros_like(state)

    x = x_ref[...]
    xext[CONV_HIST:CONV_HIST + ts, :] = xbc_ref[...]
    uext[POOL_HIST:POOL_HIST + ts, :] = u_ref[...]

    dt_raw = dtr_ref[...] + dtb_ref[...]
    dt = jnp.maximum(dt_raw, 0.0) + jnp.log1p(jnp.exp(-jnp.abs(dt_raw)))
    cs = dt * (-jnp.exp(alog_ref[...]))
    row_in_chunk = lax.broadcasted_iota(jnp.int32, (ts, LANES), 0) % CHUNK
    shift = 1
    while shift < CHUNK:
        cs = cs + jnp.where(row_in_chunk >= shift, pltpu.roll(cs, shift, axis=0), 0.0)
        shift *= 2

    slot = lax.broadcasted_iota(jnp.int32, (ts, LANES), 1) // HEAD_SLOT
    terms = _split3(dt) + _split3(cs)
    packed = jnp.zeros((ts, LANES), F32)
    for k, term in enumerate(terms):
        packed = jnp.where(slot == k, term, packed)
    ee_s[...] = _dot(packed.astype(BF16), expand_ref[...])

    conv = convb_ref[...] + convw_ref[SSD_CONV - 1:SSD_CONV, :] * xext[CONV_HIST:CONV_HIST + ts, :]
    for k in range(SSD_CONV - 1):
        off = CONV_HIST - (SSD_CONV - 1) + k
        conv = conv + convw_ref[k:k + 1, :] * xext[off:off + ts, :]
    conv = _silu(conv)
    xs_s[...] = conv[:, :SSD_DIM]
    bc_s[...] = conv[:, SSD_DIM:]
    xext[0:CONV_HIST, :] = xext[ts:ts + CONV_HIST, :]

    t_glob = sj * ts + lax.broadcasted_iota(jnp.int32, (ts, 1), 0)
    pooled = []

    def pool_group(gi):
        win = POOL_WINDOWS[gi]
        pcols = slice(gi * POOL_GROUP_DIM, (gi + 1) * POOL_GROUP_DIM)
        acc = uext[:, pcols]
        u0 = acc[POOL_HIST:, :]
        span = 1
        while span < win:
            acc = acc + pltpu.roll(acc, span, axis=0)
            span *= 2
        acc = acc[POOL_HIST:, :]
        inv_cnt = 1.0 / jnp.minimum(t_glob + 1, win).astype(F32)
        pooled.append(_dot((acc * inv_cnt - u0).astype(BF16), poolw_ref[gi]))

    sub = lax.broadcasted_iota(jnp.int32, (CHUNK, LANES), 0)
    lane = lax.broadcasted_iota(jnp.int32, (CHUNK, LANES), 1)
    diag = sub == lane % CHUNK
    causal = sub >= lane % CHUNK
    left = lane < CHUNK
    for c in range(ts // CHUNK):
        rows = slice(c * CHUNK, (c + 1) * CHUNK)
        for g in range(SSD_NGROUPS):
            gcols = slice(g * GROUP_DIM, (g + 1) * GROUP_DIM)
            xg = xs_s[rows, gcols]
            b_mat = bc_s[rows, g * SSD_STATE:(g + 1) * SSD_STATE]
            c_mat = bc_s[rows, BC_DIM + g * SSD_STATE:BC_DIM + (g + 1) * SSD_STATE]
            dt_e = ee_s[rows, gcols]
            cs_e = ee_s[rows, SSD_DIM + g * GROUP_DIM:SSD_DIM + (g + 1) * GROUP_DIM]
            cs_last = cs_e[CHUNK - 1:CHUNK, :]
            xdt = xg * dt_e
            xdt_b = xdt.astype(BF16)
            x_decayed = (xdt * jnp.exp(cs_last - cs_e)).astype(BF16)
            c_b = c_mat.astype(BF16)
            b_b = b_mat.astype(BF16)
            s_prev = state[g]
            y_off = _dot(c_b, s_prev.astype(BF16)) * jnp.exp(cs_e)
            s_new = _dot(b_mat.T.astype(BF16), x_decayed)
            state[g] = jnp.exp(cs_last) * s_prev + s_new
            cb2 = lax.dot_general(c_b, jnp.concatenate([b_b, b_b], axis=0),
                                  (((1,), (1,)), ((), ())), preferred_element_type=F32)
            for p in range(GROUP_DIM // LANES):
                pc = slice(p * LANES, (p + 1) * LANES)
                col = cs_e[:, pc]
                rowv = jnp.sum(jnp.where(diag, col, 0.0), axis=0, keepdims=True)
                decay = jnp.exp(jnp.where(causal, col - rowv, -jnp.inf))
                g_pair = (cb2 * decay).astype(BF16)
                xp = xdt_b[:, pc]
                zero = jnp.zeros_like(xp)
                block_diag = jnp.concatenate(
                    [jnp.where(left, xp, zero), jnp.where(left, zero, xp)], axis=0)
                oc = slice(g * GROUP_DIM + p * LANES, g * GROUP_DIM + (p + 1) * LANES)
                y_s[rows, oc] = (_dot(g_pair, block_diag) + y_off[:, pc]
                                 + dskip_ref[:, oc] * xg[:, pc])
        if c % 2 == 1:
            pool_group(c // 2)

    yg = y_s[...] * _silu(z_ref[...])
    normed = []
    for g in range(SSD_NGROUPS):
        seg = yg[:, g * GROUP_DIM:(g + 1) * GROUP_DIM]
        normed.append(seg * lax.rsqrt(jnp.mean(seg * seg, axis=-1, keepdims=True) + NORM_EPS))
    y_ssd = (jnp.concatenate(normed, axis=-1) * ssdn_ref[...]).astype(BF16)

    y_pool = (jnp.concatenate(pooled, axis=-1) * pools_ref[...]).astype(BF16)
    uext[0:POOL_HIST, :] = uext[ts:ts + POOL_HIST, :]

    o_ref[...] = x + _dot(jnp.concatenate([y_ssd, y_pool], axis=-1), wo_ref[...])


def _expand_matrix():
    e = np.zeros((LANES, 2 * SSD_DIM), np.float32)
    for k in range(6):
        for hd in range(SSD_HEADS):
            base = (k // 3) * SSD_DIM + hd * SSD_HEAD_DIM
            e[k * HEAD_SLOT + hd, base:base + SSD_HEAD_DIM] = 1.0
    return e


def _scan(x2d, seq_len, z, xbc, dt_raw, u, w_out, pool_w, conv_w, conv_b, dt_bias, a_log, d_skip_e,
          ssd_norm, pool_scale):
    m, d = x2d.shape
    ts = MIX_TS
    tiles_per_seq = seq_len // ts
    expand = jnp.asarray(_expand_matrix(), BF16)
    small = (conv_w, conv_b, dt_bias, a_log, d_skip_e, ssd_norm, expand, pool_scale)
    tile = lambda n: pl.BlockSpec((ts, n), lambda i: (jnp.maximum(i - SCAN_NW, 0), 0))
    n_p = len(POOL_WINDOWS)
    in_specs = [
        tile(d), tile(SSD_DIM), tile(CONV_DIM), tile(LANES), tile(POOL_DIM),
        pl.BlockSpec((MIX_WCOLS, d), lambda i: (jnp.minimum(i, SCAN_NW - 1), 0)),
        pl.BlockSpec((1, POOL_GROUP_DIM, POOL_GROUP_DIM), lambda i: (jnp.minimum(i, n_p - 1), 0, 0)),
    ]
    in_specs += [_resident(w.shape) for w in small]
    return pl.pallas_call(
        functools.partial(_scan_kernel, tiles_per_seq=tiles_per_seq),
        out_shape=jax.ShapeDtypeStruct((m, d), F32),
        grid=(SCAN_NW + m // ts,),
        in_specs=in_specs,
        out_specs=tile(d),
        scratch_shapes=[
            pltpu.VMEM((SSD_DIM + POOL_DIM, d), BF16),
            pltpu.VMEM((len(POOL_WINDOWS), POOL_GROUP_DIM, POOL_GROUP_DIM), BF16),
            pltpu.VMEM((CONV_HIST + ts, CONV_DIM), F32),
            pltpu.VMEM((POOL_HIST + ts, POOL_DIM), F32),
            pltpu.VMEM((SSD_NGROUPS, SSD_STATE, GROUP_DIM), F32),
            pltpu.VMEM((ts, SSD_DIM), F32),
            pltpu.VMEM((ts, 2 * BC_DIM), F32),
            pltpu.VMEM((ts, 2 * SSD_DIM), F32),
            pltpu.VMEM((ts, SSD_DIM), F32),
        ],
        compiler_params=pltpu.CompilerParams(
            dimension_semantics=("arbitrary",), vmem_limit_bytes=VMEM_LIMIT),
        name="scan",
    )(x2d, z, xbc, dt_raw, u, w_out, pool_w, *small)


def kernel(x, ffn1_norm, ffn1_w_gate, ffn1_w_up, ffn1_w_down, mix_norm, w_in, conv_w, conv_b,
           dt_bias, a_log, d_skip, ssd_norm, pool_w, pool_scale, w_out, ffn2_norm,
           ffn2_w_gate, ffn2_w_up, ffn2_w_down, final_norm):
    b, s, d = x.shape
    assert ffn1_norm.shape[0] == 1, "one macaron block"
    o3 = SSD_DIM + CONV_DIM + SSD_HEADS
    assert MIX_NW * MIX_WCOLS >= o3 and (MIX_NW - 1) * MIX_WCOLS == o3 - SSD_HEADS
    rep = LANES // SSD_HEADS
    row = lambda v: v.reshape(1, -1)
    ones = jnp.ones((1, d), F32)
    x2d = _ffn(x.reshape(b * s, d), row(ffn1_norm[0]), ffn1_w_gate[0], ffn1_w_up[0],
               ffn1_w_down[0], ones, final_norm=False, name="ffn1")
    z, xbc, dt_raw, u = _proj(x2d, row(mix_norm[0]), w_in[0].T, w_in[0].T[o3:])
    x2d = _scan(
        x2d, s, z, xbc, dt_raw, u, w_out[0], pool_w[0],
        conv_w[0], row(conv_b[0]),
        jnp.tile(row(dt_bias[0]), (1, rep)), jnp.tile(row(a_log[0]), (1, rep)),
        row(jnp.repeat(d_skip[0], SSD_HEAD_DIM)), row(ssd_norm[0]), row(pool_scale[0]))
    out = _ffn(x2d, row(ffn2_norm[0]), ffn2_w_gate[0], ffn2_w_up[0], ffn2_w_down[0],
               row(final_norm), final_norm=True, name="ffn2")
    return out.reshape(b, s, d)
```

```python
import functools

import numpy as np
import jax
import jax.numpy as jnp
from jax import lax
from jax.experimental import pallas as pl
from jax.experimental.pallas import tpu as pltpu

D_MODEL = 1024
CHUNK = 64
SSD_DIM = 1024
SSD_HEAD_DIM = 64
SSD_HEADS = 16
SSD_NGROUPS = 2
SSD_STATE = 128
SSD_CONV = 4
POOL_DIM = 1024
POOL_WINDOWS = (2, 4, 8, 16)
POOL_GROUP_DIM = 256
CONV_DIM = SSD_DIM + 2 * SSD_NGROUPS * SSD_STATE
D_FF = 2816
NORM_EPS = 1e-6

GROUP_DIM = SSD_DIM // SSD_NGROUPS
BC_DIM = SSD_NGROUPS * SSD_STATE
LANES = 128
SUBLANES = 8
HEAD_SLOT = 16
CONV_HIST = SUBLANES
POOL_HIST = 16

FFN_TM = 1024
FFN_TF = 256
FFN_NW = D_FF // FFN_TF
MIX_TS = 512
MIX_WCOLS = 256
MIX_NW = 11
VMEM_LIMIT = 56 * 1024 * 1024

F32 = jnp.float32
BF16 = jnp.bfloat16


def _silu(v):
    return v * jax.nn.sigmoid(v)


def _rms(v, gain):
    return v * lax.rsqrt(jnp.mean(v * v, axis=-1, keepdims=True) + NORM_EPS) * gain


def _dot(a, b):
    return jnp.dot(a, b, preferred_element_type=F32)


def _ffn_kernel(x_ref, g_ref, wg_ref, wu_ref, wd_ref, fg_ref, o_ref,
                wg_s, wu_s, wd_s, act_ref, *, final_norm):
    i = pl.program_id(0)

    @pl.when(i < FFN_NW)
    def _():
        wg_s[i] = wg_ref[...].astype(BF16)
        wu_s[i] = wu_ref[...].astype(BF16)
        wd_s[pl.ds(pl.multiple_of(i * FFN_TF, FFN_TF), FFN_TF), :] = wd_ref[...].astype(BF16)

    @pl.when(i >= FFN_NW)
    def _():
        x = x_ref[...]
        h = _rms(x, g_ref[...]).astype(BF16)
        for j in range(FFN_NW):
            gate = _dot(h, wg_s[j])
            up = _dot(h, wu_s[j])
            act_ref[:, j * FFN_TF:(j + 1) * FFN_TF] = (_silu(gate) * up).astype(BF16)
        y = x + 0.5 * _dot(act_ref[...], wd_s[...])
        if final_norm:
            y = _rms(y, fg_ref[...])
        o_ref[...] = y


def _resident(shape):
    return pl.BlockSpec(shape, lambda *_: (0,) * len(shape), pipeline_mode=pl.Buffered(1))


def _ffn(x2d, gain, w_gate, w_up, w_down, final_gain, *, final_norm, name):
    m = x2d.shape[0]
    wchunk = lambda i: jnp.minimum(i, FFN_NW - 1)
    tile = pl.BlockSpec((FFN_TM, D_MODEL), lambda i: (jnp.maximum(i - FFN_NW, 0), 0))
    return pl.pallas_call(
        functools.partial(_ffn_kernel, final_norm=final_norm),
        out_shape=jax.ShapeDtypeStruct((m, D_MODEL), F32),
        grid=(FFN_NW + m // FFN_TM,),
        in_specs=[
            tile,
            _resident((1, D_MODEL)),
            pl.BlockSpec((D_MODEL, FFN_TF), lambda i: (0, wchunk(i))),
            pl.BlockSpec((D_MODEL, FFN_TF), lambda i: (0, wchunk(i))),
            pl.BlockSpec((FFN_TF, D_MODEL), lambda i: (wchunk(i), 0)),
            _resident((1, D_MODEL)),
        ],
        out_specs=tile,
        scratch_shapes=[
            pltpu.VMEM((FFN_NW, D_MODEL, FFN_TF), BF16),
            pltpu.VMEM((FFN_NW, D_MODEL, FFN_TF), BF16),
            pltpu.VMEM((D_FF, D_MODEL), BF16),
            pltpu.VMEM((FFN_TM, D_FF), BF16),
        ],
        compiler_params=pltpu.CompilerParams(
            dimension_semantics=("arbitrary",), vmem_limit_bytes=VMEM_LIMIT),
        name=name,
    )(x2d, gain, w_gate, w_up, w_down, final_gain)


def _split3(v):
    hi = v.astype(BF16).astype(F32)
    r = v - hi
    mid = r.astype(BF16).astype(F32)
    return hi, mid, r - mid


def _mix_kernel(x_ref, g_ref, win_ref, wuf_ref, wof_ref, poolf_ref, convw_ref, convb_ref,
                dtb_ref, alog_ref, dskip_ref, ssdn_ref, expand_ref, pools_ref, o_ref,
                wz_ref, wxbc_ref, wu_ref, wo_ref,
                xext, uext, state, z_s, xs_s, bc_s, ee_s, y_s, *, tiles_per_seq):
    i = pl.program_id(0)
    z_blocks = SSD_DIM // MIX_WCOLS
    xbc_blocks = CONV_DIM // MIX_WCOLS

    for j in range(MIX_NW):
        @pl.when(i == j)
        def _(j=j):
            blk = win_ref[...].T
            if j < z_blocks:
                wz_ref[:, j * MIX_WCOLS:(j + 1) * MIX_WCOLS] = blk.astype(BF16)
            elif j < z_blocks + xbc_blocks:
                k = j - z_blocks
                wxbc_ref[:, k * MIX_WCOLS:(k + 1) * MIX_WCOLS] = blk.astype(BF16)
            else:
                lane = lax.broadcasted_iota(jnp.int32, (D_MODEL, LANES), 1)
                rep = jnp.where(lane < SSD_HEADS, blk[:, :LANES], 0.0)
                width = SSD_HEADS
                while width < LANES:
                    rep = rep + pltpu.roll(rep, width, axis=1)
                    width *= 2
                wu_ref[:, POOL_DIM:] = rep.astype(BF16)
            if j < len(POOL_WINDOWS):
                gcols = slice(j * POOL_GROUP_DIM, (j + 1) * POOL_GROUP_DIM)
                folded = _dot(wuf_ref[...].T.astype(BF16), poolf_ref[0].astype(BF16))
                wu_ref[:, gcols] = (folded * pools_ref[:, gcols]).astype(BF16)
            if j < (SSD_DIM + POOL_DIM) // MIX_WCOLS:
                wo_ref[j * MIX_WCOLS:(j + 1) * MIX_WCOLS, :] = wof_ref[...].astype(BF16)

    @pl.when(i >= MIX_NW)
    def _():
        _mix_tile(x_ref, g_ref, wz_ref, wxbc_ref, wu_ref, convw_ref, convb_ref,
                  dtb_ref, alog_ref, dskip_ref, ssdn_ref, expand_ref,
                  wo_ref, o_ref, xext, uext, state, z_s, xs_s, bc_s, ee_s, y_s,
                  (i - MIX_NW) % tiles_per_seq)


def _mix_tile(x_ref, g_ref, wz_ref, wxbc_ref, wu_ref, convw_ref, convb_ref,
              dtb_ref, alog_ref, dskip_ref, ssdn_ref, expand_ref,
              wo_ref, o_ref, xext, uext, state, z_s, xs_s, bc_s, ee_s, y_s, sj):
    ts = MIX_TS

    @pl.when(sj == 0)
    def _():
        xext[0:CONV_HIST, :] = jnp.zeros((CONV_HIST, CONV_DIM), F32)
        uext[0:POOL_HIST, :] = jnp.zeros((POOL_HIST, POOL_DIM), F32)
        state[...] = jnp.zeros_like(state)

    x = x_ref[...]
    h = _rms(x, g_ref[...]).astype(BF16)
    xext[CONV_HIST:CONV_HIST + ts, :] = _dot(h, wxbc_ref[...])
    u_dt = _dot(h, wu_ref[...])
    uext[POOL_HIST:POOL_HIST + ts, :] = u_dt[:, :POOL_DIM]

    dt_raw = u_dt[:, POOL_DIM:] + dtb_ref[...]
    dt = jnp.maximum(dt_raw, 0.0) + jnp.log1p(jnp.exp(-jnp.abs(dt_raw)))
    cs = dt * (-jnp.exp(alog_ref[...]))
    row_in_chunk = lax.broadcasted_iota(jnp.int32, (ts, LANES), 0) % CHUNK
    shift = 1
    while shift < CHUNK:
        cs = cs + jnp.where(row_in_chunk >= shift, pltpu.roll(cs, shift, axis=0), 0.0)
        shift *= 2

    slot = lax.broadcasted_iota(jnp.int32, (ts, LANES), 1) // HEAD_SLOT
    terms = _split3(dt) + _split3(cs)
    packed = jnp.zeros((ts, LANES), F32)
    for k, term in enumerate(terms):
        packed = jnp.where(slot == k, term, packed)
    ee_s[...] = _dot(packed.astype(BF16), expand_ref[...])

    conv = convb_ref[...] + convw_ref[SSD_CONV - 1:SSD_CONV, :] * xext[CONV_HIST:CONV_HIST + ts, :]
    for k in range(SSD_CONV - 1):
        off = CONV_HIST - (SSD_CONV - 1) + k
        conv = conv + convw_ref[k:k + 1, :] * xext[off:off + ts, :]
    conv = _silu(conv)
    xs_s[...] = conv[:, :SSD_DIM]
    bc_s[...] = conv[:, SSD_DIM:]
    xext[0:CONV_HIST, :] = xext[ts:ts + CONV_HIST, :]

    t_glob = sj * ts + lax.broadcasted_iota(jnp.int32, (ts, 1), 0)
    pooled = []

    def pool_group(gi):
        win = POOL_WINDOWS[gi]
        pcols = slice(gi * POOL_GROUP_DIM, (gi + 1) * POOL_GROUP_DIM)
        acc = uext[:, pcols]
        u0 = acc[POOL_HIST:, :]
        span = 1
        while span < win:
            acc = acc + pltpu.roll(acc, span, axis=0)
            span *= 2
        acc = acc[POOL_HIST:, :]
        inv_cnt = 1.0 / jnp.minimum(t_glob + 1, win).astype(F32)
        pooled.append((acc * inv_cnt - u0).astype(BF16))

    sub = lax.broadcasted_iota(jnp.int32, (CHUNK, LANES), 0)
    lane = lax.broadcasted_iota(jnp.int32, (CHUNK, LANES), 1)
    diag = sub == lane % CHUNK
    causal = sub >= lane % CHUNK
    left = lane < CHUNK
    for c in range(ts // CHUNK):
        rows = slice(c * CHUNK, (c + 1) * CHUNK)
        for g in range(SSD_NGROUPS):
            gcols = slice(g * GROUP_DIM, (g + 1) * GROUP_DIM)
            xg = xs_s[rows, gcols]
            b_mat = bc_s[rows, g * SSD_STATE:(g + 1) * SSD_STATE]
            c_mat = bc_s[rows, BC_DIM + g * SSD_STATE:BC_DIM + (g + 1) * SSD_STATE]
            dt_e = ee_s[rows, gcols]
            cs_e = ee_s[rows, SSD_DIM + g * GROUP_DIM:SSD_DIM + (g + 1) * GROUP_DIM]
            cs_last = cs_e[CHUNK - 1:CHUNK, :]
            xdt = xg * dt_e
            xdt_b = xdt.astype(BF16)
            x_decayed = (xdt * jnp.exp(cs_last - cs_e)).astype(BF16)
            c_b = c_mat.astype(BF16)
            b_b = b_mat.astype(BF16)
            s_prev = state[g]
            y_off = _dot(c_b, s_prev.astype(BF16)) * jnp.exp(cs_e)
            s_new = _dot(b_mat.T.astype(BF16), x_decayed)
            state[g] = jnp.exp(cs_last) * s_prev + s_new
            cb2 = lax.dot_general(c_b, jnp.concatenate([b_b, b_b], axis=0),
                                  (((1,), (1,)), ((), ())), preferred_element_type=F32)
            for p in range(GROUP_DIM // LANES):
                pc = slice(p * LANES, (p + 1) * LANES)
                col = cs_e[:, pc]
                rowv = jnp.sum(jnp.where(diag, col, 0.0), axis=0, keepdims=True)
                decay = jnp.exp(jnp.where(causal, col - rowv, -jnp.inf))
                g_pair = (cb2 * decay).astype(BF16)
                xp = xdt_b[:, pc]
                zero = jnp.zeros_like(xp)
                block_diag = jnp.concatenate(
                    [jnp.where(left, xp, zero), jnp.where(left, zero, xp)], axis=0)
                oc = slice(g * GROUP_DIM + p * LANES, g * GROUP_DIM + (p + 1) * LANES)
                y_s[rows, oc] = (_dot(g_pair, block_diag) + y_off[:, pc]
                                 + dskip_ref[:, oc] * xg[:, pc])
        if c % 2 == 1:
            pool_group(c // 2)

    z_s[...] = _dot(h, wz_ref[...])
    yg = y_s[...] * _silu(z_s[...])
    normed = []
    for g in range(SSD_NGROUPS):
        seg = yg[:, g * GROUP_DIM:(g + 1) * GROUP_DIM]
        normed.append(seg * lax.rsqrt(jnp.mean(seg * seg, axis=-1, keepdims=True) + NORM_EPS))
    y_ssd = (jnp.concatenate(normed, axis=-1) * ssdn_ref[...]).astype(BF16)

    y_pool = jnp.concatenate(pooled, axis=-1)
    uext[0:POOL_HIST, :] = uext[ts:ts + POOL_HIST, :]

    o_ref[...] = x + _dot(jnp.concatenate([y_ssd, y_pool], axis=-1), wo_ref[...])


def _expand_matrix():
    e = np.zeros((LANES, 2 * SSD_DIM), np.float32)
    for k in range(6):
        for hd in range(SSD_HEADS):
            base = (k // 3) * SSD_DIM + hd * SSD_HEAD_DIM
            e[k * HEAD_SLOT + hd, base:base + SSD_HEAD_DIM] = 1.0
    return e


def _mix(x2d, seq_len, gain, w_in, w_u, w_out, pool_w, conv_w, conv_b, dt_bias, a_log, d_skip_e,
         ssd_norm, pool_scale):
    m, d = x2d.shape
    ts = MIX_TS
    tiles_per_seq = seq_len // ts
    expand = jnp.asarray(_expand_matrix(), BF16)
    small = (conv_w, conv_b, dt_bias, a_log, d_skip_e, ssd_norm, expand, pool_scale)
    tile = pl.BlockSpec((ts, d), lambda i: (jnp.maximum(i - MIX_NW, 0), 0))
    n_u = POOL_DIM // MIX_WCOLS
    n_o = (SSD_DIM + POOL_DIM) // MIX_WCOLS
    n_p = len(POOL_WINDOWS)
    in_specs = [
        tile,
        _resident(gain.shape),
        pl.BlockSpec((MIX_WCOLS, d), lambda i: (jnp.minimum(i, MIX_NW - 1), 0)),
        pl.BlockSpec((MIX_WCOLS, d), lambda i: (jnp.minimum(i, n_u - 1), 0)),
        pl.BlockSpec((MIX_WCOLS, d), lambda i: (jnp.minimum(i, n_o - 1), 0)),
        pl.BlockSpec((1, POOL_GROUP_DIM, POOL_GROUP_DIM), lambda i: (jnp.minimum(i, n_p - 1), 0, 0)),
    ]
    in_specs += [_resident(w.shape) for w in small]
    return pl.pallas_call(
        functools.partial(_mix_kernel, tiles_per_seq=tiles_per_seq),
        out_shape=jax.ShapeDtypeStruct((m, d), F32),
        grid=(MIX_NW + m // ts,),
        in_specs=in_specs,
        out_specs=tile,
        scratch_shapes=[
            pltpu.VMEM((d, SSD_DIM), BF16),
            pltpu.VMEM((d, CONV_DIM), BF16),
            pltpu.VMEM((d, POOL_DIM + LANES), BF16),
            pltpu.VMEM((SSD_DIM + POOL_DIM, d), BF16),
            pltpu.VMEM((CONV_HIST + ts, CONV_DIM), F32),
            pltpu.VMEM((POOL_HIST + ts, POOL_DIM), F32),
            pltpu.VMEM((SSD_NGROUPS, SSD_STATE, GROUP_DIM), F32),
            pltpu.VMEM((ts, SSD_DIM), F32),
            pltpu.VMEM((ts, SSD_DIM), F32),
            pltpu.VMEM((ts, 2 * BC_DIM), F32),
            pltpu.VMEM((ts, 2 * SSD_DIM), F32),
            pltpu.VMEM((ts, SSD_DIM), F32),
        ],
        compiler_params=pltpu.CompilerParams(
            dimension_semantics=("arbitrary",), vmem_limit_bytes=VMEM_LIMIT),
        name="mix",
    )(x2d, gain, w_in, w_u, w_out, pool_w, *small)


def kernel(x, ffn1_norm, ffn1_w_gate, ffn1_w_up, ffn1_w_down, mix_norm, w_in, conv_w, conv_b,
           dt_bias, a_log, d_skip, ssd_norm, pool_w, pool_scale, w_out, ffn2_norm,
           ffn2_w_gate, ffn2_w_up, ffn2_w_down, final_norm):
    b, s, d = x.shape
    assert ffn1_norm.shape[0] == 1, "one macaron block"
    o3 = SSD_DIM + CONV_DIM + SSD_HEADS
    assert MIX_NW * MIX_WCOLS >= o3 and (MIX_NW - 1) * MIX_WCOLS == o3 - SSD_HEADS
    rep = LANES // SSD_HEADS
    row = lambda v: v.reshape(1, -1)
    ones = jnp.ones((1, d), F32)
    x2d = _ffn(x.reshape(b * s, d), row(ffn1_norm[0]), ffn1_w_gate[0], ffn1_w_up[0],
               ffn1_w_down[0], ones, final_norm=False, name="ffn1")
    x2d = _mix(
        x2d, s, row(mix_norm[0]), w_in[0].T, w_in[0].T[o3:], w_out[0], pool_w[0],
        conv_w[0], row(conv_b[0]),
        jnp.tile(row(dt_bias[0]), (1, rep)), jnp.tile(row(a_log[0]), (1, rep)),
        row(jnp.repeat(d_skip[0], SSD_HEAD_DIM)), row(ssd_norm[0]), row(pool_scale[0]))
    out = _ffn(x2d, row(ffn2_norm[0]), ffn2_w_gate[0], ffn2_w_up[0], ffn2_w_down[0],
               row(final_norm), final_norm=True, name="ffn2")
    return out.reshape(b, s, d)
```

```python
import functools

import numpy as np
import jax
import jax.numpy as jnp
from jax import lax
from jax.experimental import pallas as pl
from jax.experimental.pallas import tpu as pltpu

D_MODEL = 1024
CHUNK = 64
SSD_DIM = 1024
SSD_HEAD_DIM = 64
SSD_HEADS = 16
SSD_NGROUPS = 2
SSD_STATE = 128
SSD_CONV = 4
POOL_DIM = 1024
POOL_WINDOWS = (2, 4, 8, 16)
POOL_GROUP_DIM = 256
CONV_DIM = SSD_DIM + 2 * SSD_NGROUPS * SSD_STATE
D_FF = 2816
NORM_EPS = 1e-6

GROUP_DIM = SSD_DIM // SSD_NGROUPS
BC_DIM = SSD_NGROUPS * SSD_STATE
LANES = 128
SUBLANES = 8
HEAD_SLOT = 16
CONV_HIST = SUBLANES
POOL_HIST = 16

FFN_TM = 1024
FFN_TF = 256
FFN_NW = D_FF // FFN_TF
MIX_TS = 512
MIX_WCOLS = 256
MIX_NW = 11
VMEM_LIMIT = 56 * 1024 * 1024

F32 = jnp.float32
BF16 = jnp.bfloat16


def _silu(v):
    return v * jax.nn.sigmoid(v)


def _unit_rms(v):
    return v * lax.rsqrt(jnp.mean(v * v, axis=-1, keepdims=True) + NORM_EPS)


def _rms(v, gain):
    return _unit_rms(v) * gain


def _dot(a, b):
    return jnp.dot(a, b, preferred_element_type=F32)


def _ffn_kernel(x_ref, gcol_ref, wg_ref, wu_ref, wd_ref, fg_ref, o_ref,
                wg_s, wu_s, wd_s, act_ref, *, final_norm):
    i = pl.program_id(0)

    @pl.when(i < FFN_NW)
    def _():
        gcol = gcol_ref[...]
        wg_s[i] = (wg_ref[...] * gcol).astype(BF16)
        wu_s[i] = (wu_ref[...] * gcol).astype(BF16)
        wd_s[pl.ds(pl.multiple_of(i * FFN_TF, FFN_TF), FFN_TF), :] = (
            0.5 * wd_ref[...]).astype(BF16)

    @pl.when(i >= FFN_NW)
    def _():
        x = x_ref[...]
        h = _unit_rms(x).astype(BF16)
        for j in range(FFN_NW):
            gate = _dot(h, wg_s[j])
            up = _dot(h, wu_s[j])
            act_ref[:, j * FFN_TF:(j + 1) * FFN_TF] = (_silu(gate) * up).astype(BF16)
        y = x + _dot(act_ref[...], wd_s[...])
        if final_norm:
            y = _rms(y, fg_ref[...])
        o_ref[...] = y


def _resident(shape):
    return pl.BlockSpec(shape, lambda *_: (0,) * len(shape), pipeline_mode=pl.Buffered(1))


def _ffn(x2d, gain_col, w_gate, w_up, w_down, final_gain, *, final_norm, name):
    m = x2d.shape[0]
    wchunk = lambda i: jnp.minimum(i, FFN_NW - 1)
    tile = pl.BlockSpec((FFN_TM, D_MODEL), lambda i: (jnp.maximum(i - FFN_NW, 0), 0))
    return pl.pallas_call(
        functools.partial(_ffn_kernel, final_norm=final_norm),
        out_shape=jax.ShapeDtypeStruct((m, D_MODEL), F32),
        grid=(FFN_NW + m // FFN_TM,),
        in_specs=[
            tile,
            _resident((D_MODEL, 1)),
            pl.BlockSpec((D_MODEL, FFN_TF), lambda i: (0, wchunk(i))),
            pl.BlockSpec((D_MODEL, FFN_TF), lambda i: (0, wchunk(i))),
            pl.BlockSpec((FFN_TF, D_MODEL), lambda i: (wchunk(i), 0)),
            _resident((1, D_MODEL)),
        ],
        out_specs=tile,
        scratch_shapes=[
            pltpu.VMEM((FFN_NW, D_MODEL, FFN_TF), BF16),
            pltpu.VMEM((FFN_NW, D_MODEL, FFN_TF), BF16),
            pltpu.VMEM((D_FF, D_MODEL), BF16),
            pltpu.VMEM((FFN_TM, D_FF), BF16),
        ],
        compiler_params=pltpu.CompilerParams(
            dimension_semantics=("arbitrary",), vmem_limit_bytes=VMEM_LIMIT),
        name=name,
    )(x2d, gain_col, w_gate, w_up, w_down, final_gain)


def _split3(v):
    hi = v.astype(BF16).astype(F32)
    r = v - hi
    mid = r.astype(BF16).astype(F32)
    return hi, mid, r - mid


def _mix_kernel(x_ref, g_ref, win_ref, wuf_ref, wof_ref, poolf_ref, convw_ref, convb_ref,
                dtb_ref, alog_ref, dskip_ref, ssdn_ref, expand_ref, pools_ref, o_ref,
                wz_ref, wxbc_ref, wu_ref, wo_ref,
                xext, uext, state, z_s, xs_s, bc_s, ee_s, y_s, *, tiles_per_seq):
    i = pl.program_id(0)
    z_blocks = SSD_DIM // MIX_WCOLS
    xbc_blocks = CONV_DIM // MIX_WCOLS

    for j in range(MIX_NW):
        @pl.when(i == j)
        def _(j=j):
            blk = (win_ref[...] * g_ref[...]).T
            if j < z_blocks:
                wz_ref[:, j * MIX_WCOLS:(j + 1) * MIX_WCOLS] = blk.astype(BF16)
            elif j < z_blocks + xbc_blocks:
                k = j - z_blocks
                wxbc_ref[:, k * MIX_WCOLS:(k + 1) * MIX_WCOLS] = blk.astype(BF16)
            else:
                lane = lax.broadcasted_iota(jnp.int32, (D_MODEL, LANES), 1)
                rep = jnp.where(lane < SSD_HEADS, blk[:, :LANES], 0.0)
                width = SSD_HEADS
                while width < LANES:
                    rep = rep + pltpu.roll(rep, width, axis=1)
                    width *= 2
                wu_ref[:, POOL_DIM:] = rep.astype(BF16)
            if j < len(POOL_WINDOWS):
                gcols = slice(j * POOL_GROUP_DIM, (j + 1) * POOL_GROUP_DIM)
                folded = _dot((wuf_ref[...] * g_ref[...]).T.astype(BF16), poolf_ref[0].astype(BF16))
                wu_ref[:, gcols] = (folded * pools_ref[:, gcols]).astype(BF16)
            if j < SSD_DIM // MIX_WCOLS:
                gain_rows = ssdn_ref[j * MIX_WCOLS:(j + 1) * MIX_WCOLS, :]
                wo_ref[j * MIX_WCOLS:(j + 1) * MIX_WCOLS, :] = (wof_ref[...] * gain_rows).astype(BF16)
            elif j < (SSD_DIM + POOL_DIM) // MIX_WCOLS:
                wo_ref[j * MIX_WCOLS:(j + 1) * MIX_WCOLS, :] = wof_ref[...].astype(BF16)

    @pl.when(i >= MIX_NW)
    def _():
        _mix_tile(x_ref, wz_ref, wxbc_ref, wu_ref, convw_ref, convb_ref,
                  dtb_ref, alog_ref, dskip_ref, expand_ref,
                  wo_ref, o_ref, xext, uext, state, z_s, xs_s, bc_s, ee_s, y_s,
                  (i - MIX_NW) % tiles_per_seq)


def _mix_tile(x_ref, wz_ref, wxbc_ref, wu_ref, convw_ref, convb_ref,
              dtb_ref, alog_ref, dskip_ref, expand_ref,
              wo_ref, o_ref, xext, uext, state, z_s, xs_s, bc_s, ee_s, y_s, sj):
    ts = MIX_TS

    @pl.when(sj == 0)
    def _():
        xext[0:CONV_HIST, :] = jnp.zeros((CONV_HIST, CONV_DIM), F32)
        uext[0:POOL_HIST, :] = jnp.zeros((POOL_HIST, POOL_DIM), F32)
        state[...] = jnp.zeros_like(state)

    x = x_ref[...]
    h = _unit_rms(x).astype(BF16)
    xext[CONV_HIST:CONV_HIST + ts, :] = _dot(h, wxbc_ref[...])
    u_dt = _dot(h, wu_ref[...])
    uext[POOL_HIST:POOL_HIST + ts, :] = u_dt[:, :POOL_DIM]

    dt_raw = u_dt[:, POOL_DIM:] + dtb_ref[...]
    dt = jnp.maximum(dt_raw, 0.0) + jnp.log1p(jnp.exp(-jnp.abs(dt_raw)))
    cs = dt * (-jnp.exp(alog_ref[...]))
    row_in_chunk = lax.broadcasted_iota(jnp.int32, (ts, LANES), 0) % CHUNK
    shift = 1
    while shift < CHUNK:
        cs = cs + jnp.where(row_in_chunk >= shift, pltpu.roll(cs, shift, axis=0), 0.0)
        shift *= 2

    slot = lax.broadcasted_iota(jnp.int32, (ts, LANES), 1) // HEAD_SLOT
    terms = _split3(dt) + _split3(cs)
    packed = jnp.zeros((ts, LANES), F32)
    for k, term in enumerate(terms):
        packed = jnp.where(slot == k, term, packed)
    ee_s[...] = _dot(packed.astype(BF16), expand_ref[...])

    conv = convb_ref[...] + convw_ref[SSD_CONV - 1:SSD_CONV, :] * xext[CONV_HIST:CONV_HIST + ts, :]
    for k in range(SSD_CONV - 1):
        off = CONV_HIST - (SSD_CONV - 1) + k
        conv = conv + convw_ref[k:k + 1, :] * xext[off:off + ts, :]
    conv = _silu(conv)
    xs_s[...] = conv[:, :SSD_DIM]
    bc_s[...] = conv[:, SSD_DIM:]
    xext[0:CONV_HIST, :] = xext[ts:ts + CONV_HIST, :]

    t_glob = sj * ts + lax.broadcasted_iota(jnp.int32, (ts, 1), 0)
    pooled = []

    def pool_group(gi):
        win = POOL_WINDOWS[gi]
        pcols = slice(gi * POOL_GROUP_DIM, (gi + 1) * POOL_GROUP_DIM)
        acc = uext[:, pcols]
        u0 = acc[POOL_HIST:, :]
        span = 1
        while span < win:
            acc = acc + pltpu.roll(acc, span, axis=0)
            span *= 2
        acc = acc[POOL_HIST:, :]
        inv_cnt = 1.0 / jnp.minimum(t_glob + 1, win).astype(F32)
        pooled.append((acc * inv_cnt - u0).astype(BF16))

    sub = lax.broadcasted_iota(jnp.int32, (CHUNK, LANES), 0)
    lane = lax.broadcasted_iota(jnp.int32, (CHUNK, LANES), 1)
    diag = sub == lane % CHUNK
    causal = sub >= lane % CHUNK
    left = lane < CHUNK
    for c in range(ts // CHUNK):
        rows = slice(c * CHUNK, (c + 1) * CHUNK)
        for g in range(SSD_NGROUPS):
            gcols = slice(g * GROUP_DIM, (g + 1) * GROUP_DIM)
            xg = xs_s[rows, gcols]
            b_mat = bc_s[rows, g * SSD_STATE:(g + 1) * SSD_STATE]
            c_mat = bc_s[rows, BC_DIM + g * SSD_STATE:BC_DIM + (g + 1) * SSD_STATE]
            dt_e = ee_s[rows, gcols]
            cs_e = ee_s[rows, SSD_DIM + g * GROUP_DIM:SSD_DIM + (g + 1) * GROUP_DIM]
            cs_last = cs_e[CHUNK - 1:CHUNK, :]
            xdt = xg * dt_e
            xdt_b = xdt.astype(BF16)
            x_decayed = (xdt * jnp.exp(cs_last - cs_e)).astype(BF16)
            c_b = c_mat.astype(BF16)
            b_b = b_mat.astype(BF16)
            s_prev = state[g]
            y_off = _dot(c_b, s_prev.astype(BF16)) * jnp.exp(cs_e)
            s_new = _dot(b_mat.T.astype(BF16), x_decayed)
            state[g] = jnp.exp(cs_last) * s_prev + s_new
            cb2 = lax.dot_general(c_b, jnp.concatenate([b_b, b_b], axis=0),
                                  (((1,), (1,)), ((), ())), preferred_element_type=F32)
            for p in range(GROUP_DIM // LANES):
                pc = slice(p * LANES, (p + 1) * LANES)
                col = cs_e[:, pc]
                rowv = jnp.sum(jnp.where(diag, col, 0.0), axis=0, keepdims=True)
                decay = jnp.exp(jnp.where(causal, col - rowv, -jnp.inf))
                g_pair = (cb2 * decay).astype(BF16)
                xp = xdt_b[:, pc]
                zero = jnp.zeros_like(xp)
                block_diag = jnp.concatenate(
                    [jnp.where(left, xp, zero), jnp.where(left, zero, xp)], axis=0)
                oc = slice(g * GROUP_DIM + p * LANES, g * GROUP_DIM + (p + 1) * LANES)
                y_s[rows, oc] = (_dot(g_pair, block_diag) + y_off[:, pc]
                                 + dskip_ref[:, oc] * xg[:, pc])
        if c % 2 == 1:
            pool_group(c // 2)

    z_s[...] = _dot(h, wz_ref[...])
    yg = y_s[...] * _silu(z_s[...])
    normed = []
    for g in range(SSD_NGROUPS):
        seg = yg[:, g * GROUP_DIM:(g + 1) * GROUP_DIM]
        normed.append(seg * lax.rsqrt(jnp.mean(seg * seg, axis=-1, keepdims=True) + NORM_EPS))
    y_ssd = jnp.concatenate(normed, axis=-1).astype(BF16)

    y_pool = jnp.concatenate(pooled, axis=-1)
    uext[0:POOL_HIST, :] = uext[ts:ts + POOL_HIST, :]

    o_ref[...] = x + _dot(jnp.concatenate([y_ssd, y_pool], axis=-1), wo_ref[...])


def _expand_matrix():
    e = np.zeros((LANES, 2 * SSD_DIM), np.float32)
    for k in range(6):
        for hd in range(SSD_HEADS):
            base = (k // 3) * SSD_DIM + hd * SSD_HEAD_DIM
            e[k * HEAD_SLOT + hd, base:base + SSD_HEAD_DIM] = 1.0
    return e


def _mix(x2d, seq_len, gain, w_in, w_u, w_out, pool_w, conv_w, conv_b, dt_bias, a_log, d_skip_e,
         ssd_norm, pool_scale):
    m, d = x2d.shape
    ts = MIX_TS
    tiles_per_seq = seq_len // ts
    expand = jnp.asarray(_expand_matrix(), BF16)
    small = (conv_w, conv_b, dt_bias, a_log, d_skip_e, ssd_norm, expand, pool_scale)
    tile = pl.BlockSpec((ts, d), lambda i: (jnp.maximum(i - MIX_NW, 0), 0))
    n_u = POOL_DIM // MIX_WCOLS
    n_o = (SSD_DIM + POOL_DIM) // MIX_WCOLS
    n_p = len(POOL_WINDOWS)
    in_specs = [
        tile,
        _resident(gain.shape),
        pl.BlockSpec((MIX_WCOLS, d), lambda i: (jnp.minimum(i, MIX_NW - 1), 0)),
        pl.BlockSpec((MIX_WCOLS, d), lambda i: (jnp.minimum(i, n_u - 1), 0)),
        pl.BlockSpec((MIX_WCOLS, d), lambda i: (jnp.minimum(i, n_o - 1), 0)),
        pl.BlockSpec((1, POOL_GROUP_DIM, POOL_GROUP_DIM), lambda i: (jnp.minimum(i, n_p - 1), 0, 0)),
    ]
    in_specs += [_resident(w.shape) for w in small]
    return pl.pallas_call(
        functools.partial(_mix_kernel, tiles_per_seq=tiles_per_seq),
        out_shape=jax.ShapeDtypeStruct((m, d), F32),
        grid=(MIX_NW + m // ts,),
        in_specs=in_specs,
        out_specs=tile,
        scratch_shapes=[
            pltpu.VMEM((d, SSD_DIM), BF16),
            pltpu.VMEM((d, CONV_DIM), BF16),
            pltpu.VMEM((d, POOL_DIM + LANES), BF16),
            pltpu.VMEM((SSD_DIM + POOL_DIM, d), BF16),
            pltpu.VMEM((CONV_HIST + ts, CONV_DIM), F32),
            pltpu.VMEM((POOL_HIST + ts, POOL_DIM), F32),
            pltpu.VMEM((SSD_NGROUPS, SSD_STATE, GROUP_DIM), F32),
            pltpu.VMEM((ts, SSD_DIM), F32),
            pltpu.VMEM((ts, SSD_DIM), F32),
            pltpu.VMEM((ts, 2 * BC_DIM), F32),
            pltpu.VMEM((ts, 2 * SSD_DIM), F32),
            pltpu.VMEM((ts, SSD_DIM), F32),
        ],
        compiler_params=pltpu.CompilerParams(
            dimension_semantics=("arbitrary",), vmem_limit_bytes=VMEM_LIMIT),
        name="mix",
    )(x2d, gain, w_in, w_u, w_out, pool_w, *small)


def kernel(x, ffn1_norm, ffn1_w_gate, ffn1_w_up, ffn1_w_down, mix_norm, w_in, conv_w, conv_b,
           dt_bias, a_log, d_skip, ssd_norm, pool_w, pool_scale, w_out, ffn2_norm,
           ffn2_w_gate, ffn2_w_up, ffn2_w_down, final_norm):
    b, s, d = x.shape
    assert ffn1_norm.shape[0] == 1, "one macaron block"
    o3 = SSD_DIM + CONV_DIM + SSD_HEADS
    assert MIX_NW * MIX_WCOLS >= o3 and (MIX_NW - 1) * MIX_WCOLS == o3 - SSD_HEADS
    rep = LANES // SSD_HEADS
    row = lambda v: v.reshape(1, -1)
    col = lambda v: v.reshape(-1, 1)
    ones = jnp.ones((1, d), F32)
    x2d = _ffn(x.reshape(b * s, d), col(ffn1_norm[0]), ffn1_w_gate[0], ffn1_w_up[0],
               ffn1_w_down[0], ones, final_norm=False, name="ffn1")
    x2d = _mix(
        x2d, s, row(mix_norm[0]), w_in[0].T, w_in[0].T[o3:], w_out[0], pool_w[0],
        conv_w[0], row(conv_b[0]),
        jnp.tile(row(dt_bias[0]), (1, rep)), jnp.tile(row(a_log[0]), (1, rep)),
        row(jnp.repeat(d_skip[0], SSD_HEAD_DIM)), col(ssd_norm[0]), row(pool_scale[0]))
    out = _ffn(x2d, col(ffn2_norm[0]), ffn2_w_gate[0], ffn2_w_up[0], ffn2_w_down[0],
               row(final_norm), final_norm=True, name="ffn2")
    return out.reshape(b, s, d)
```

```python
import functools

import numpy as np
import jax
import jax.numpy as jnp
from jax import lax
from jax.experimental import pallas as pl
from jax.experimental.pallas import tpu as pltpu

D_MODEL = 1024
CHUNK = 64
SSD_DIM = 1024
SSD_HEAD_DIM = 64
SSD_HEADS = 16
SSD_NGROUPS = 2
SSD_STATE = 128
SSD_CONV = 4
POOL_DIM = 1024
POOL_WINDOWS = (2, 4, 8, 16)
POOL_GROUP_DIM = 256
CONV_DIM = SSD_DIM + 2 * SSD_NGROUPS * SSD_STATE
D_FF = 2816
NORM_EPS = 1e-6

GROUP_DIM = SSD_DIM // SSD_NGROUPS
BC_DIM = SSD_NGROUPS * SSD_STATE
LANES = 128
SUBLANES = 8
HEAD_SLOT = 16
CONV_HIST = SUBLANES
POOL_HIST = 16

FFN_TM = 1024
FFN_TF = 256
FFN_NW = D_FF // FFN_TF
MIX_TS = 512
MIX_WCOLS = 256
MIX_NW = 11
VMEM_LIMIT = 56 * 1024 * 1024

F32 = jnp.float32
BF16 = jnp.bfloat16


def _silu(v):
    return v * jax.nn.sigmoid(v)


def _rms(v, gain):
    return v * lax.rsqrt(jnp.mean(v * v, axis=-1, keepdims=True) + NORM_EPS) * gain


def _dot(a, b):
    return jnp.dot(a, b, preferred_element_type=F32)


def _ffn_kernel(x_ref, g_ref, wg_ref, wu_ref, wd_ref, fg_ref, o_ref,
                wg_s, wu_s, wd_s, act_ref, *, final_norm):
    i = pl.program_id(0)

    @pl.when(i < FFN_NW)
    def _():
        wg_s[i] = wg_ref[...].astype(BF16)
        wu_s[i] = wu_ref[...].astype(BF16)
        wd_s[pl.ds(pl.multiple_of(i * FFN_TF, FFN_TF), FFN_TF), :] = wd_ref[...].astype(BF16)

    @pl.when(i >= FFN_NW)
    def _():
        x = x_ref[...]
        h = _rms(x, g_ref[...]).astype(BF16)
        for j in range(FFN_NW):
            gate = _dot(h, wg_s[j])
            up = _dot(h, wu_s[j])
            act_ref[:, j * FFN_TF:(j + 1) * FFN_TF] = (_silu(gate) * up).astype(BF16)
        y = x + 0.5 * _dot(act_ref[...], wd_s[...])
        if final_norm:
            y = _rms(y, fg_ref[...])
        o_ref[...] = y


def _resident(shape):
    return pl.BlockSpec(shape, lambda *_: (0,) * len(shape), pipeline_mode=pl.Buffered(1))


def _ffn(x2d, gain, w_gate, w_up, w_down, final_gain, *, final_norm, name):
    m = x2d.shape[0]
    wchunk = lambda i: jnp.minimum(i, FFN_NW - 1)
    tile = pl.BlockSpec((FFN_TM, D_MODEL), lambda i: (jnp.maximum(i - FFN_NW, 0), 0))
    return pl.pallas_call(
        functools.partial(_ffn_kernel, final_norm=final_norm),
        out_shape=jax.ShapeDtypeStruct((m, D_MODEL), F32),
        grid=(FFN_NW + m // FFN_TM,),
        in_specs=[
            tile,
            _resident((1, D_MODEL)),
            pl.BlockSpec((D_MODEL, FFN_TF), lambda i: (0, wchunk(i))),
            pl.BlockSpec((D_MODEL, FFN_TF), lambda i: (0, wchunk(i))),
            pl.BlockSpec((FFN_TF, D_MODEL), lambda i: (wchunk(i), 0)),
            _resident((1, D_MODEL)),
        ],
        out_specs=tile,
        scratch_shapes=[
            pltpu.VMEM((FFN_NW, D_MODEL, FFN_TF), BF16),
            pltpu.VMEM((FFN_NW, D_MODEL, FFN_TF), BF16),
            pltpu.VMEM((D_FF, D_MODEL), BF16),
            pltpu.VMEM((FFN_TM, D_FF), BF16),
        ],
        compiler_params=pltpu.CompilerParams(
            dimension_semantics=("arbitrary",), vmem_limit_bytes=VMEM_LIMIT),
        name=name,
    )(x2d, gain, w_gate, w_up, w_down, final_gain)


def _split3(v):
    hi = v.astype(BF16).astype(F32)
    r = v - hi
    mid = r.astype(BF16).astype(F32)
    return hi, mid, r - mid


def _mix_kernel(x_ref, g_ref, win_ref, wuf_ref, wof_ref, poolf_ref, convw_ref, convb_ref,
                dtb_ref, alog_ref, dskip_ref, ssdn_ref, expand_ref, pools_ref, o_ref,
                wz_ref, wxbc_ref, wu_ref, wo_ref,
                xext, uext, state, z_s, xs_s, bc_s, ee_s, y_s, *, tiles_per_seq):
    i = pl.program_id(0)
    z_blocks = SSD_DIM // MIX_WCOLS
    xbc_blocks = CONV_DIM // MIX_WCOLS

    for j in range(MIX_NW):
        @pl.when(i == j)
        def _(j=j):
            blk = win_ref[...].T
            if j < z_blocks:
                wz_ref[:, j * MIX_WCOLS:(j + 1) * MIX_WCOLS] = blk.astype(BF16)
            elif j < z_blocks + xbc_blocks:
                k = j - z_blocks
                wxbc_ref[:, k * MIX_WCOLS:(k + 1) * MIX_WCOLS] = blk.astype(BF16)
            else:
                lane = lax.broadcasted_iota(jnp.int32, (D_MODEL, LANES), 1)
                rep = jnp.where(lane < SSD_HEADS, blk[:, :LANES], 0.0)
                width = SSD_HEADS
                while width < LANES:
                    rep = rep + pltpu.roll(rep, width, axis=1)
                    width *= 2
                wu_ref[:, POOL_DIM:] = rep.astype(BF16)
            if j < len(POOL_WINDOWS):
                gcols = slice(j * POOL_GROUP_DIM, (j + 1) * POOL_GROUP_DIM)
                folded = _dot(wuf_ref[...].T.astype(BF16), poolf_ref[0].astype(BF16))
                wu_ref[:, gcols] = (folded * pools_ref[:, gcols]).astype(BF16)
            if j < (SSD_DIM + POOL_DIM) // MIX_WCOLS:
                wo_ref[j * MIX_WCOLS:(j + 1) * MIX_WCOLS, :] = wof_ref[...].astype(BF16)

    @pl.when(i >= MIX_NW)
    def _():
        _mix_tile(x_ref, g_ref, wz_ref, wxbc_ref, wu_ref, convw_ref, convb_ref,
                  dtb_ref, alog_ref, dskip_ref, ssdn_ref, expand_ref,
                  wo_ref, o_ref, xext, uext, state, z_s, xs_s, bc_s, ee_s, y_s,
                  (i - MIX_NW) % tiles_per_seq)


def _mix_tile(x_ref, g_ref, wz_ref, wxbc_ref, wu_ref, convw_ref, convb_ref,
              dtb_ref, alog_ref, dskip_ref, ssdn_ref, expand_ref,
              wo_ref, o_ref, xext, uext, state, z_s, xs_s, bc_s, ee_s, y_s, sj):
    ts = MIX_TS

    @pl.when(sj == 0)
    def _():
        xext[0:CONV_HIST, :] = jnp.zeros((CONV_HIST, CONV_DIM), F32)
        uext[0:POOL_HIST, :] = jnp.zeros((POOL_HIST, POOL_DIM), F32)
        state[...] = jnp.zeros_like(state)

    x = x_ref[...]
    h = _rms(x, g_ref[...]).astype(BF16)
    xext[CONV_HIST:CONV_HIST + ts, :] = _dot(h, wxbc_ref[...])
    u_dt = _dot(h, wu_ref[...])
    uext[POOL_HIST:POOL_HIST + ts, :] = u_dt[:, :POOL_DIM]

    dt_raw = u_dt[:, POOL_DIM:] + dtb_ref[...]
    dt = jnp.maximum(dt_raw, 0.0) + jnp.log1p(jnp.exp(-jnp.abs(dt_raw)))
    cs = dt * (-jnp.exp(alog_ref[...]))
    row_in_chunk = lax.broadcasted_iota(jnp.int32, (ts, LANES), 0) % CHUNK
    shift = 1
    while shift < CHUNK:
        cs = cs + jnp.where(row_in_chunk >= shift, pltpu.roll(cs, shift, axis=0), 0.0)
        shift *= 2

    slot = lax.broadcasted_iota(jnp.int32, (ts, LANES), 1) // HEAD_SLOT
    terms = _split3(dt) + _split3(cs)
    packed = jnp.zeros((ts, LANES), F32)
    for k, term in enumerate(terms):
        packed = jnp.where(slot == k, term, packed)
    ee_s[...] = _dot(packed.astype(BF16), expand_ref[...])

    conv = convb_ref[...] + convw_ref[SSD_CONV - 1:SSD_CONV, :] * xext[CONV_HIST:CONV_HIST + ts, :]
    for k in range(SSD_CONV - 1):
        off = CONV_HIST - (SSD_CONV - 1) + k
        conv = conv + convw_ref[k:k + 1, :] * xext[off:off + ts, :]
    conv = _silu(conv)
    xs_s[...] = conv[:, :SSD_DIM]
    bc_s[...] = conv[:, SSD_DIM:]
    xext[0:CONV_HIST, :] = xext[ts:ts + CONV_HIST, :]

    t_glob = sj * ts + lax.broadcasted_iota(jnp.int32, (ts, 1), 0)
    pooled = []

    def pool_group(gi):
        win = POOL_WINDOWS[gi]
        pcols = slice(gi * POOL_GROUP_DIM, (gi + 1) * POOL_GROUP_DIM)
        acc = uext[:, pcols]
        u0 = acc[POOL_HIST:, :]
        span = 1
        while span < win:
            acc = acc + pltpu.roll(acc, span, axis=0)
            span *= 2
        acc = acc[POOL_HIST:, :]
        inv_cnt = 1.0 / jnp.minimum(t_glob + 1, win).astype(F32)
        pooled.append((acc * inv_cnt - u0).astype(BF16))

    sub = lax.broadcasted_iota(jnp.int32, (CHUNK, LANES), 0)
    lane = lax.broadcasted_iota(jnp.int32, (CHUNK, LANES), 1)
    diag = sub == lane % CHUNK
    causal = sub >= lane % CHUNK
    left = lane < CHUNK
    for c in range(ts // CHUNK):
        rows = slice(c * CHUNK, (c + 1) * CHUNK)
        for g in range(SSD_NGROUPS):
            gcols = slice(g * GROUP_DIM, (g + 1) * GROUP_DIM)
            xg = xs_s[rows, gcols]
            b_mat = bc_s[rows, g * SSD_STATE:(g + 1) * SSD_STATE]
            c_mat = bc_s[rows, BC_DIM + g * SSD_STATE:BC_DIM + (g + 1) * SSD_STATE]
            dt_e = ee_s[rows, gcols]
            cs_e = ee_s[rows, SSD_DIM + g * GROUP_DIM:SSD_DIM + (g + 1) * GROUP_DIM]
            cs_last = cs_e[CHUNK - 1:CHUNK, :]
            xdt = xg * dt_e
            xdt_b = xdt.astype(BF16)
            x_decayed = (xdt * jnp.exp(cs_last - cs_e)).astype(BF16)
            c_b = c_mat.astype(BF16)
            b_b = b_mat.astype(BF16)
            s_prev = state[g]
            y_off = _dot(c_b, s_prev.astype(BF16)) * jnp.exp(cs_e)
            s_new = _dot(b_mat.T.astype(BF16), x_decayed)
            state[g] = jnp.exp(cs_last) * s_prev + s_new
            cb2 = lax.dot_general(c_b, jnp.concatenate([b_b, b_b], axis=0),
                                  (((1,), (1,)), ((), ())), preferred_element_type=F32)
            for quad in range(GROUP_DIM // (2 * LANES)):
                scores, blocks = [], []
                for half in range(2):
                    p = 2 * quad + half
                    pc = slice(p * LANES, (p + 1) * LANES)
                    col = cs_e[:, pc]
                    rowv = jnp.sum(jnp.where(diag, col, 0.0), axis=0, keepdims=True)
                    decay = jnp.exp(jnp.where(causal, col - rowv, -jnp.inf))
                    scores.append((cb2 * decay).astype(BF16))
                    xp = xdt_b[:, pc]
                    zero = jnp.zeros_like(xp)
                    for blk in (jnp.where(left, xp, zero), jnp.where(left, zero, xp)):
                        blocks.append(jnp.concatenate([blk, zero] if half == 0 else [zero, blk], axis=-1))
                qc = slice(2 * quad * LANES, 2 * (quad + 1) * LANES)
                oc = slice(g * GROUP_DIM + qc.start, g * GROUP_DIM + qc.stop)
                y_s[rows, oc] = (_dot(jnp.concatenate(scores, axis=-1), jnp.concatenate(blocks, axis=0))
                                 + y_off[:, qc] + dskip_ref[:, oc] * xg[:, qc])
        if c % 2 == 1:
            pool_group(c // 2)

    z_s[...] = _dot(h, wz_ref[...])
    yg = y_s[...] * _silu(z_s[...])
    normed = []
    for g in range(SSD_NGROUPS):
        seg = yg[:, g * GROUP_DIM:(g + 1) * GROUP_DIM]
        normed.append(seg * lax.rsqrt(jnp.mean(seg * seg, axis=-1, keepdims=True) + NORM_EPS))
    y_ssd = (jnp.concatenate(normed, axis=-1) * ssdn_ref[...]).astype(BF16)

    y_pool = jnp.concatenate(pooled, axis=-1)
    uext[0:POOL_HIST, :] = uext[ts:ts + POOL_HIST, :]

    o_ref[...] = x + _dot(jnp.concatenate([y_ssd, y_pool], axis=-1), wo_ref[...])


def _expand_matrix():
    e = np.zeros((LANES, 2 * SSD_DIM), np.float32)
    for k in range(6):
        for hd in range(SSD_HEADS):
            base = (k // 3) * SSD_DIM + hd * SSD_HEAD_DIM
            e[k * HEAD_SLOT + hd, base:base + SSD_HEAD_DIM] = 1.0
    return e


def _mix(x2d, seq_len, gain, w_in, w_u, w_out, pool_w, conv_w, conv_b, dt_bias, a_log, d_skip_e,
         ssd_norm, pool_scale):
    m, d = x2d.shape
    ts = MIX_TS
    tiles_per_seq = seq_len // ts
    expand = jnp.asarray(_expand_matrix(), BF16)
    small = (conv_w, conv_b, dt_bias, a_log, d_skip_e, ssd_norm, expand, pool_scale)
    tile = pl.BlockSpec((ts, d), lambda i: (jnp.maximum(i - MIX_NW, 0), 0))
    n_u = POOL_DIM // MIX_WCOLS
    n_o = (SSD_DIM + POOL_DIM) // MIX_WCOLS
    n_p = len(POOL_WINDOWS)
    in_specs = [
        tile,
        _resident(gain.shape),
        pl.BlockSpec((MIX_WCOLS, d), lambda i: (jnp.minimum(i, MIX_NW - 1), 0)),
        pl.BlockSpec((MIX_WCOLS, d), lambda i: (jnp.minimum(i, n_u - 1), 0)),
        pl.BlockSpec((MIX_WCOLS, d), lambda i: (jnp.minimum(i, n_o - 1), 0)),
        pl.BlockSpec((1, POOL_GROUP_DIM, POOL_GROUP_DIM), lambda i: (jnp.minimum(i, n_p - 1), 0, 0)),
    ]
    in_specs += [_resident(w.shape) for w in small]
    return pl.pallas_call(
        functools.partial(_mix_kernel, tiles_per_seq=tiles_per_seq),
        out_shape=jax.ShapeDtypeStruct((m, d), F32),
        grid=(MIX_NW + m // ts,),
        in_specs=in_specs,
        out_specs=tile,
        scratch_shapes=[
            pltpu.VMEM((d, SSD_DIM), BF16),
            pltpu.VMEM((d, CONV_DIM), BF16),
            pltpu.VMEM((d, POOL_DIM + LANES), BF16),
            pltpu.VMEM((SSD_DIM + POOL_DIM, d), BF16),
            pltpu.VMEM((CONV_HIST + ts, CONV_DIM), F32),
            pltpu.VMEM((POOL_HIST + ts, POOL_DIM), F32),
            pltpu.VMEM((SSD_NGROUPS, SSD_STATE, GROUP_DIM), F32),
            pltpu.VMEM((ts, SSD_DIM), F32),
            pltpu.VMEM((ts, SSD_DIM), F32),
            pltpu.VMEM((ts, 2 * BC_DIM), F32),
            pltpu.VMEM((ts, 2 * SSD_DIM), F32),
            pltpu.VMEM((ts, SSD_DIM), F32),
        ],
        compiler_params=pltpu.CompilerParams(
            dimension_semantics=("arbitrary",), vmem_limit_bytes=VMEM_LIMIT),
        name="mix",
    )(x2d, gain, w_in, w_u, w_out, pool_w, *small)


def kernel(x, ffn1_norm, ffn1_w_gate, ffn1_w_up, ffn1_w_down, mix_norm, w_in, conv_w, conv_b,
           dt_bias, a_log, d_skip, ssd_norm, pool_w, pool_scale, w_out, ffn2_norm,
           ffn2_w_gate, ffn2_w_up, ffn2_w_down, final_norm):
    b, s, d = x.shape
    assert ffn1_norm.shape[0] == 1, "one macaron block"
    o3 = SSD_DIM + CONV_DIM + SSD_HEADS
    assert MIX_NW * MIX_WCOLS >= o3 and (MIX_NW - 1) * MIX_WCOLS == o3 - SSD_HEADS
    rep = LANES // SSD_HEADS
    row = lambda v: v.reshape(1, -1)
    ones = jnp.ones((1, d), F32)
    x2d = _ffn(x.reshape(b * s, d), row(ffn1_norm[0]), ffn1_w_gate[0], ffn1_w_up[0],
               ffn1_w_down[0], ones, final_norm=False, name="ffn1")
    x2d = _mix(
        x2d, s, row(mix_norm[0]), w_in[0].T, w_in[0].T[o3:], w_out[0], pool_w[0],
        conv_w[0], row(conv_b[0]),
        jnp.tile(row(dt_bias[0]), (1, rep)), jnp.tile(row(a_log[0]), (1, rep)),
        row(jnp.repeat(d_skip[0], SSD_HEAD_DIM)), row(ssd_norm[0]), row(pool_scale[0]))
    out = _ffn(x2d, row(ffn2_norm[0]), ffn2_w_gate[0], ffn2_w_up[0], ffn2_w_down[0],
               row(final_norm), final_norm=True, name="ffn2")
    return out.reshape(b, s, d)
```

```python
import functools

import numpy as np
import jax
import jax.numpy as jnp
from jax import lax
from jax.experimental import pallas as pl
from jax.experimental.pallas import tpu as pltpu

D_MODEL = 1024
CHUNK = 64
SSD_DIM = 1024
SSD_HEAD_DIM = 64
SSD_HEADS = 16
SSD_NGROUPS = 2
SSD_STATE = 128
SSD_CONV = 4
POOL_DIM = 1024
POOL_WINDOWS = (2, 4, 8, 16)
POOL_GROUP_DIM = 256
CONV_DIM = SSD_DIM + 2 * SSD_NGROUPS * SSD_STATE
D_FF = 2816
NORM_EPS = 1e-6

GROUP_DIM = SSD_DIM // SSD_NGROUPS
BC_DIM = SSD_NGROUPS * SSD_STATE
LANES = 128
SUBLANES = 8
HEAD_SLOT = 16
CONV_HIST = SUBLANES
POOL_HIST = 16

FFN_TM = 1024
FFN_TF = 256
FFN_NW = D_FF // FFN_TF
MIX_TS = 512
MIX_WCOLS = 256
MIX_NW = 11
VMEM_LIMIT = 56 * 1024 * 1024

F32 = jnp.float32
BF16 = jnp.bfloat16


def _silu(v):
    return v * jax.nn.sigmoid(v)


def _rms(v, gain):
    return v * lax.rsqrt(jnp.mean(v * v, axis=-1, keepdims=True) + NORM_EPS) * gain


def _dot(a, b):
    return jnp.dot(a, b, preferred_element_type=F32)


def _ffn_kernel(x_ref, g_ref, wg_ref, wu_ref, wd_ref, fg_ref, o_ref,
                wg_s, wu_s, wd_s, act_ref, *, final_norm):
    i = pl.program_id(0)

    @pl.when(i < FFN_NW)
    def _():
        wg_s[i] = wg_ref[...].astype(BF16)
        wu_s[i] = wu_ref[...].astype(BF16)
        wd_s[pl.ds(pl.multiple_of(i * FFN_TF, FFN_TF), FFN_TF), :] = wd_ref[...].astype(BF16)

    @pl.when(i >= FFN_NW)
    def _():
        x = x_ref[...]
        h = _rms(x, g_ref[...]).astype(BF16)
        for j in range(FFN_NW):
            gate = _dot(h, wg_s[j])
            up = _dot(h, wu_s[j])
            act_ref[:, j * FFN_TF:(j + 1) * FFN_TF] = (_silu(gate) * up).astype(BF16)
        y = x + 0.5 * _dot(act_ref[...], wd_s[...])
        if final_norm:
            y = _rms(y, fg_ref[...])
        o_ref[...] = y


def _resident(shape):
    return pl.BlockSpec(shape, lambda *_: (0,) * len(shape), pipeline_mode=pl.Buffered(1))


def _ffn(x2d, gain, w_gate, w_up, w_down, final_gain, *, final_norm, name):
    m = x2d.shape[0]
    wchunk = lambda i: jnp.minimum(i, FFN_NW - 1)
    tile = pl.BlockSpec((FFN_TM, D_MODEL), lambda i: (jnp.maximum(i - FFN_NW, 0), 0))
    return pl.pallas_call(
        functools.partial(_ffn_kernel, final_norm=final_norm),
        out_shape=jax.ShapeDtypeStruct((m, D_MODEL), F32),
        grid=(FFN_NW + m // FFN_TM,),
        in_specs=[
            tile,
            _resident((1, D_MODEL)),
            pl.BlockSpec((D_MODEL, FFN_TF), lambda i: (0, wchunk(i))),
            pl.BlockSpec((D_MODEL, FFN_TF), lambda i: (0, wchunk(i))),
            pl.BlockSpec((FFN_TF, D_MODEL), lambda i: (wchunk(i), 0)),
            _resident((1, D_MODEL)),
        ],
        out_specs=tile,
        scratch_shapes=[
            pltpu.VMEM((FFN_NW, D_MODEL, FFN_TF), BF16),
            pltpu.VMEM((FFN_NW, D_MODEL, FFN_TF), BF16),
            pltpu.VMEM((D_FF, D_MODEL), BF16),
            pltpu.VMEM((FFN_TM, D_FF), BF16),
        ],
        compiler_params=pltpu.CompilerParams(
            dimension_semantics=("arbitrary",), vmem_limit_bytes=VMEM_LIMIT),
        name=name,
    )(x2d, gain, w_gate, w_up, w_down, final_gain)


def _split3(v):
    hi = v.astype(BF16).astype(F32)
    r = v - hi
    mid = r.astype(BF16).astype(F32)
    return hi, mid, r - mid


def _mix_kernel(x_ref, g_ref, win_ref, wuf_ref, wof_ref, poolf_ref, convw_ref, convb_ref,
                dtb_ref, alog_ref, dskip_ref, ssdn_ref, expand_ref, pools_ref, o_ref,
                wz_ref, wxbc_ref, wu_ref, wo_ref,
                xext, uext, state, z_s, xs_s, bc_s, ee_s, y_s, *, tiles_per_seq):
    i = pl.program_id(0)
    z_blocks = SSD_DIM // MIX_WCOLS
    xbc_blocks = CONV_DIM // MIX_WCOLS

    for j in range(MIX_NW):
        @pl.when(i == j)
        def _(j=j):
            blk = win_ref[...].T
            if j < z_blocks:
                wz_ref[:, j * MIX_WCOLS:(j + 1) * MIX_WCOLS] = blk.astype(BF16)
            elif j < z_blocks + xbc_blocks:
                k = j - z_blocks
                wxbc_ref[:, k * MIX_WCOLS:(k + 1) * MIX_WCOLS] = blk.astype(BF16)
            else:
                lane = lax.broadcasted_iota(jnp.int32, (D_MODEL, LANES), 1)
                rep = jnp.where(lane < SSD_HEADS, blk[:, :LANES], 0.0)
                width = SSD_HEADS
                while width < LANES:
                    rep = rep + pltpu.roll(rep, width, axis=1)
                    width *= 2
                wu_ref[:, POOL_DIM:] = rep.astype(BF16)
            if j < len(POOL_WINDOWS):
                gcols = slice(j * POOL_GROUP_DIM, (j + 1) * POOL_GROUP_DIM)
                folded = _dot(wuf_ref[...].T.astype(BF16), poolf_ref[0].astype(BF16))
                wu_ref[:, gcols] = (folded * pools_ref[:, gcols]).astype(BF16)
            if j < (SSD_DIM + POOL_DIM) // MIX_WCOLS:
                wo_ref[j * MIX_WCOLS:(j + 1) * MIX_WCOLS, :] = wof_ref[...].astype(BF16)

    @pl.when(i >= MIX_NW)
    def _():
        _mix_tile(x_ref, g_ref, wz_ref, wxbc_ref, wu_ref, convw_ref, convb_ref,
                  dtb_ref, alog_ref, dskip_ref, ssdn_ref, expand_ref,
                  wo_ref, o_ref, xext, uext, state, z_s, xs_s, bc_s, ee_s, y_s,
                  (i - MIX_NW) % tiles_per_seq)


def _mix_tile(x_ref, g_ref, wz_ref, wxbc_ref, wu_ref, convw_ref, convb_ref,
              dtb_ref, alog_ref, dskip_ref, ssdn_ref, expand_ref,
              wo_ref, o_ref, xext, uext, state, z_s, xs_s, bc_s, ee_s, y_s, sj):
    ts = MIX_TS

    @pl.when(sj == 0)
    def _():
        xext[0:CONV_HIST, :] = jnp.zeros((CONV_HIST, CONV_DIM), F32)
        uext[0:POOL_HIST, :] = jnp.zeros((POOL_HIST, POOL_DIM), F32)
        state[...] = jnp.zeros_like(state)

    x = x_ref[...]
    h = _rms(x, g_ref[...]).astype(BF16)
    xext[CONV_HIST:CONV_HIST + ts, :] = _dot(h, wxbc_ref[...])
    u_dt = _dot(h, wu_ref[...])
    uext[POOL_HIST:POOL_HIST + ts, :] = u_dt[:, :POOL_DIM]

    dt_raw = u_dt[:, POOL_DIM:] + dtb_ref[...]
    dt = jnp.maximum(dt_raw, 0.0) + jnp.log1p(jnp.exp(-jnp.abs(dt_raw)))
    cs = dt * (-jnp.exp(alog_ref[...]))
    row_in_chunk = lax.broadcasted_iota(jnp.int32, (ts, LANES), 0) % CHUNK
    shift = 1
    while shift < CHUNK:
        cs = cs + jnp.where(row_in_chunk >= shift, pltpu.roll(cs, shift, axis=0), 0.0)
        shift *= 2

    slot = lax.broadcasted_iota(jnp.int32, (ts, LANES), 1) // HEAD_SLOT
    terms = _split3(dt) + _split3(cs)
    packed = jnp.zeros((ts, LANES), F32)
    for k, term in enumerate(terms):
        packed = jnp.where(slot == k, term, packed)
    ee_s[...] = _dot(packed.astype(BF16), expand_ref[...])

    conv = convb_ref[...] + convw_ref[SSD_CONV - 1:SSD_CONV, :] * xext[CONV_HIST:CONV_HIST + ts, :]
    for k in range(SSD_CONV - 1):
        off = CONV_HIST - (SSD_CONV - 1) + k
        conv = conv + convw_ref[k:k + 1, :] * xext[off:off + ts, :]
    conv = _silu(conv)
    xs_s[...] = conv[:, :SSD_DIM]
    bc_s[...] = conv[:, SSD_DIM:]
    xext[0:CONV_HIST, :] = xext[ts:ts + CONV_HIST, :]

    t_glob = sj * ts + lax.broadcasted_iota(jnp.int32, (ts, 1), 0)
    pooled = []

    def pool_group(gi):
        win = POOL_WINDOWS[gi]
        pcols = slice(gi * POOL_GROUP_DIM, (gi + 1) * POOL_GROUP_DIM)
        acc = uext[:, pcols]
        u0 = acc[POOL_HIST:, :]
        span = 1
        while span < win:
            acc = acc + pltpu.roll(acc, span, axis=0)
            span *= 2
        acc = acc[POOL_HIST:, :]
        inv_cnt = 1.0 / jnp.minimum(t_glob + 1, win).astype(F32)
        pooled.append((acc * inv_cnt - u0).astype(BF16))

    sub = lax.broadcasted_iota(jnp.int32, (CHUNK, LANES), 0)
    lane = lax.broadcasted_iota(jnp.int32, (CHUNK, LANES), 1)
    diag = sub == lane % CHUNK
    causal = sub >= lane % CHUNK
    left = lane < CHUNK
    for c in range(ts // CHUNK):
        rows = slice(c * CHUNK, (c + 1) * CHUNK)
        b_all = bc_s[rows, 0:BC_DIM]
        cb_lhs = bc_s[rows, BC_DIM:2 * BC_DIM].astype(BF16)
        zero_b = jnp.zeros((2 * CHUNK, SSD_STATE), BF16)
        cb_rhs = []
        for g in range(SSD_NGROUPS):
            b_b = b_all[:, g * SSD_STATE:(g + 1) * SSD_STATE].astype(BF16)
            b2 = jnp.concatenate([b_b, b_b], axis=0)
            cb_rhs.append(jnp.concatenate(
                [b2 if k == g else zero_b for k in range(SSD_NGROUPS)], axis=-1))
        cb_all = lax.dot_general(cb_lhs, jnp.concatenate(cb_rhs, axis=0),
                                 (((1,), (1,)), ((), ())), preferred_element_type=F32)
        for g in range(SSD_NGROUPS):
            gcols = slice(g * GROUP_DIM, (g + 1) * GROUP_DIM)
            xg = xs_s[rows, gcols]
            b_mat = b_all[:, g * SSD_STATE:(g + 1) * SSD_STATE]
            c_b = cb_lhs[:, g * SSD_STATE:(g + 1) * SSD_STATE]
            cb2 = cb_all[:, g * 2 * CHUNK:(g + 1) * 2 * CHUNK]
            dt_e = ee_s[rows, gcols]
            cs_e = ee_s[rows, SSD_DIM + g * GROUP_DIM:SSD_DIM + (g + 1) * GROUP_DIM]
            cs_last = cs_e[CHUNK - 1:CHUNK, :]
            xdt = xg * dt_e
            xdt_b = xdt.astype(BF16)
            x_decayed = (xdt * jnp.exp(cs_last - cs_e)).astype(BF16)
            s_prev = state[g]
            y_off = _dot(c_b, s_prev.astype(BF16)) * jnp.exp(cs_e)
            s_new = _dot(b_mat.T.astype(BF16), x_decayed)
            state[g] = jnp.exp(cs_last) * s_prev + s_new
            for quad in range(GROUP_DIM // (2 * LANES)):
                scores, blocks = [], []
                for half in range(2):
                    p = 2 * quad + half
                    pc = slice(p * LANES, (p + 1) * LANES)
                    col = cs_e[:, pc]
                    rowv = jnp.sum(jnp.where(diag, col, 0.0), axis=0, keepdims=True)
                    decay = jnp.exp(jnp.where(causal, col - rowv, -jnp.inf))
                    scores.append((cb2 * decay).astype(BF16))
                    xp = xdt_b[:, pc]
                    zero = jnp.zeros_like(xp)
                    for blk in (jnp.where(left, xp, zero), jnp.where(left, zero, xp)):
                        blocks.append(jnp.concatenate([blk, zero] if half == 0 else [zero, blk], axis=-1))
                qc = slice(2 * quad * LANES, 2 * (quad + 1) * LANES)
                oc = slice(g * GROUP_DIM + qc.start, g * GROUP_DIM + qc.stop)
                y_s[rows, oc] = (_dot(jnp.concatenate(scores, axis=-1), jnp.concatenate(blocks, axis=0))
                                 + y_off[:, qc] + dskip_ref[:, oc] * xg[:, qc])
        if c % 2 == 1:
            pool_group(c // 2)

    z_s[...] = _dot(h, wz_ref[...])
    yg = y_s[...] * _silu(z_s[...])
    normed = []
    for g in range(SSD_NGROUPS):
        seg = yg[:, g * GROUP_DIM:(g + 1) * GROUP_DIM]
        normed.append(seg * lax.rsqrt(jnp.mean(seg * seg, axis=-1, keepdims=True) + NORM_EPS))
    y_ssd = (jnp.concatenate(normed, axis=-1) * ssdn_ref[...]).astype(BF16)

    y_pool = jnp.concatenate(pooled, axis=-1)
    uext[0:POOL_HIST, :] = uext[ts:ts + POOL_HIST, :]

    o_ref[...] = x + _dot(jnp.concatenate([y_ssd, y_pool], axis=-1), wo_ref[...])


def _expand_matrix():
    e = np.zeros((LANES, 2 * SSD_DIM), np.float32)
    for k in range(6):
        for hd in range(SSD_HEADS):
            base = (k // 3) * SSD_DIM + hd * SSD_HEAD_DIM
            e[k * HEAD_SLOT + hd, base:base + SSD_HEAD_DIM] = 1.0
    return e


def _mix(x2d, seq_len, gain, w_in, w_u, w_out, pool_w, conv_w, conv_b, dt_bias, a_log, d_skip_e,
         ssd_norm, pool_scale):
    m, d = x2d.shape
    ts = MIX_TS
    tiles_per_seq = seq_len // ts
    expand = jnp.asarray(_expand_matrix(), BF16)
    small = (conv_w, conv_b, dt_bias, a_log, d_skip_e, ssd_norm, expand, pool_scale)
    tile = pl.BlockSpec((ts, d), lambda i: (jnp.maximum(i - MIX_NW, 0), 0))
    n_u = POOL_DIM // MIX_WCOLS
    n_o = (SSD_DIM + POOL_DIM) // MIX_WCOLS
    n_p = len(POOL_WINDOWS)
    in_specs = [
        tile,
        _resident(gain.shape),
        pl.BlockSpec((MIX_WCOLS, d), lambda i: (jnp.minimum(i, MIX_NW - 1), 0)),
        pl.BlockSpec((MIX_WCOLS, d), lambda i: (jnp.minimum(i, n_u - 1), 0)),
        pl.BlockSpec((MIX_WCOLS, d), lambda i: (jnp.minimum(i, n_o - 1), 0)),
        pl.BlockSpec((1, POOL_GROUP_DIM, POOL_GROUP_DIM), lambda i: (jnp.minimum(i, n_p - 1), 0, 0)),
    ]
    in_specs += [_resident(w.shape) for w in small]
    return pl.pallas_call(
        functools.partial(_mix_kernel, tiles_per_seq=tiles_per_seq),
        out_shape=jax.ShapeDtypeStruct((m, d), F32),
        grid=(MIX_NW + m // ts,),
        in_specs=in_specs,
        out_specs=tile,
        scratch_shapes=[
            pltpu.VMEM((d, SSD_DIM), BF16),
            pltpu.VMEM((d, CONV_DIM), BF16),
            pltpu.VMEM((d, POOL_DIM + LANES), BF16),
            pltpu.VMEM((SSD_DIM + POOL_DIM, d), BF16),
            pltpu.VMEM((CONV_HIST + ts, CONV_DIM), F32),
            pltpu.VMEM((POOL_HIST + ts, POOL_DIM), F32),
            pltpu.VMEM((SSD_NGROUPS, SSD_STATE, GROUP_DIM), F32),
            pltpu.VMEM((ts, SSD_DIM), F32),
            pltpu.VMEM((ts, SSD_DIM), F32),
            pltpu.VMEM((ts, 2 * BC_DIM), F32),
            pltpu.VMEM((ts, 2 * SSD_DIM), F32),
            pltpu.VMEM((ts, SSD_DIM), F32),
        ],
        compiler_params=pltpu.CompilerParams(
            dimension_semantics=("arbitrary",), vmem_limit_bytes=VMEM_LIMIT),
        name="mix",
    )(x2d, gain, w_in, w_u, w_out, pool_w, *small)


def kernel(x, ffn1_norm, ffn1_w_gate, ffn1_w_up, ffn1_w_down, mix_norm, w_in, conv_w, conv_b,
           dt_bias, a_log, d_skip, ssd_norm, pool_w, pool_scale, w_out, ffn2_norm,
           ffn2_w_gate, ffn2_w_up, ffn2_w_down, final_norm):
    b, s, d = x.shape
    assert ffn1_norm.shape[0] == 1, "one macaron block"
    o3 = SSD_DIM + CONV_DIM + SSD_HEADS
    assert MIX_NW * MIX_WCOLS >= o3 and (MIX_NW - 1) * MIX_WCOLS == o3 - SSD_HEADS
    rep = LANES // SSD_HEADS
    row = lambda v: v.reshape(1, -1)
    ones = jnp.ones((1, d), F32)
    x2d = _ffn(x.reshape(b * s, d), row(ffn1_norm[0]), ffn1_w_gate[0], ffn1_w_up[0],
               ffn1_w_down[0], ones, final_norm=False, name="ffn1")
    x2d = _mix(
        x2d, s, row(mix_norm[0]), w_in[0].T, w_in[0].T[o3:], w_out[0], pool_w[0],
        conv_w[0], row(conv_b[0]),
        jnp.tile(row(dt_bias[0]), (1, rep)), jnp.tile(row(a_log[0]), (1, rep)),
        row(jnp.repeat(d_skip[0], SSD_HEAD_DIM)), row(ssd_norm[0]), row(pool_scale[0]))
    out = _ffn(x2d, row(ffn2_norm[0]), ffn2_w_gate[0], ffn2_w_up[0], ffn2_w_down[0],
               row(final_norm), final_norm=True, name="ffn2")
    return out.reshape(b, s, d)
```

```python
import functools

import numpy as np
import jax
import jax.numpy as jnp
from jax import lax
from jax.experimental import pallas as pl
from jax.experimental.pallas import tpu as pltpu

D_MODEL = 1024
CHUNK = 64
SSD_DIM = 1024
SSD_HEAD_DIM = 64
SSD_HEADS = 16
SSD_NGROUPS = 2
SSD_STATE = 128
SSD_CONV = 4
POOL_DIM = 1024
POOL_WINDOWS = (2, 4, 8, 16)
POOL_GROUP_DIM = 256
CONV_DIM = SSD_DIM + 2 * SSD_NGROUPS * SSD_STATE
D_FF = 2816
NORM_EPS = 1e-6

GROUP_DIM = SSD_DIM // SSD_NGROUPS
BC_DIM = SSD_NGROUPS * SSD_STATE
LANES = 128
SUBLANES = 8
HEAD_SLOT = 16
CONV_HIST = SUBLANES
POOL_HIST = 16

FFN_TM = 1024
FFN_TF = 256
FFN_NW = D_FF // FFN_TF
MIX_TS = 512
MIX_WCOLS = 256
MIX_NW = 11
VMEM_LIMIT = 56 * 1024 * 1024

F32 = jnp.float32
BF16 = jnp.bfloat16


def _silu(v):
    return v * jax.nn.sigmoid(v)


def _rms(v, gain):
    return v * lax.rsqrt(jnp.mean(v * v, axis=-1, keepdims=True) + NORM_EPS) * gain


def _dot(a, b):
    return jnp.dot(a, b, preferred_element_type=F32)


def _ffn_kernel(x_ref, g_ref, wg_ref, wu_ref, wd_ref, fg_ref, o_ref,
                wg_s, wu_s, wd_s, act_ref, *, final_norm):
    i = pl.program_id(0)

    @pl.when(i < FFN_NW)
    def _():
        wg_s[i] = wg_ref[...].astype(BF16)
        wu_s[i] = wu_ref[...].astype(BF16)
        wd_s[pl.ds(pl.multiple_of(i * FFN_TF, FFN_TF), FFN_TF), :] = wd_ref[...].astype(BF16)

    @pl.when(i >= FFN_NW)
    def _():
        x = x_ref[...]
        h = _rms(x, g_ref[...]).astype(BF16)
        for j in range(FFN_NW):
            gate = _dot(h, wg_s[j])
            up = _dot(h, wu_s[j])
            act_ref[:, j * FFN_TF:(j + 1) * FFN_TF] = (_silu(gate) * up).astype(BF16)
        y = x + 0.5 * _dot(act_ref[...], wd_s[...])
        if final_norm:
            y = _rms(y, fg_ref[...])
        o_ref[...] = y


def _resident(shape):
    return pl.BlockSpec(shape, lambda *_: (0,) * len(shape), pipeline_mode=pl.Buffered(1))


def _ffn(x2d, gain, w_gate, w_up, w_down, final_gain, *, final_norm, name):
    m = x2d.shape[0]
    wchunk = lambda i: jnp.minimum(i, FFN_NW - 1)
    tile = pl.BlockSpec((FFN_TM, D_MODEL), lambda i: (jnp.maximum(i - FFN_NW, 0), 0))
    return pl.pallas_call(
        functools.partial(_ffn_kernel, final_norm=final_norm),
        out_shape=jax.ShapeDtypeStruct((m, D_MODEL), F32),
        grid=(FFN_NW + m // FFN_TM,),
        in_specs=[
            tile,
            _resident((1, D_MODEL)),
            pl.BlockSpec((D_MODEL, FFN_TF), lambda i: (0, wchunk(i))),
            pl.BlockSpec((D_MODEL, FFN_TF), lambda i: (0, wchunk(i))),
            pl.BlockSpec((FFN_TF, D_MODEL), lambda i: (wchunk(i), 0)),
            _resident((1, D_MODEL)),
        ],
        out_specs=tile,
        scratch_shapes=[
            pltpu.VMEM((FFN_NW, D_MODEL, FFN_TF), BF16),
            pltpu.VMEM((FFN_NW, D_MODEL, FFN_TF), BF16),
            pltpu.VMEM((D_FF, D_MODEL), BF16),
            pltpu.VMEM((FFN_TM, D_FF), BF16),
        ],
        compiler_params=pltpu.CompilerParams(
            dimension_semantics=("arbitrary",), vmem_limit_bytes=VMEM_LIMIT),
        name=name,
    )(x2d, gain, w_gate, w_up, w_down, final_gain)


def _split3(v):
    hi = v.astype(BF16).astype(F32)
    r = v - hi
    mid = r.astype(BF16).astype(F32)
    return hi, mid, r - mid


def _mix_kernel(x_ref, g_ref, win_ref, wuf_ref, wof_ref, poolf_ref, convw_ref, convb_ref,
                dtb_ref, alog_ref, dskip_ref, ssdn_ref, expand_ref, pools_ref, o_ref,
                wz_ref, wxbc_ref, wu_ref, wo_ref,
                xext, uext, state, z_s, xs_s, bc_s, ee_s, y_s, *, tiles_per_seq):
    i = pl.program_id(0)
    z_blocks = SSD_DIM // MIX_WCOLS
    xbc_blocks = CONV_DIM // MIX_WCOLS

    for j in range(MIX_NW):
        @pl.when(i == j)
        def _(j=j):
            blk = win_ref[...].T
            if j < z_blocks:
                wz_ref[:, j * MIX_WCOLS:(j + 1) * MIX_WCOLS] = blk.astype(BF16)
            elif j < z_blocks + xbc_blocks:
                k = j - z_blocks
                wxbc_ref[:, k * MIX_WCOLS:(k + 1) * MIX_WCOLS] = blk.astype(BF16)
            else:
                lane = lax.broadcasted_iota(jnp.int32, (D_MODEL, LANES), 1)
                rep = jnp.where(lane < SSD_HEADS, blk[:, :LANES], 0.0)
                width = SSD_HEADS
                while width < LANES:
                    rep = rep + pltpu.roll(rep, width, axis=1)
                    width *= 2
                wu_ref[:, POOL_DIM:] = rep.astype(BF16)
            if j < len(POOL_WINDOWS):
                gcols = slice(j * POOL_GROUP_DIM, (j + 1) * POOL_GROUP_DIM)
                folded = _dot(wuf_ref[...].T.astype(BF16), poolf_ref[0].astype(BF16))
                wu_ref[:, gcols] = (folded * pools_ref[:, gcols]).astype(BF16)
            if j < (SSD_DIM + POOL_DIM) // MIX_WCOLS:
                wo_ref[j * MIX_WCOLS:(j + 1) * MIX_WCOLS, :] = wof_ref[...].astype(BF16)

    @pl.when(i >= MIX_NW)
    def _():
        _mix_tile(x_ref, g_ref, wz_ref, wxbc_ref, wu_ref, convw_ref, convb_ref,
                  dtb_ref, alog_ref, dskip_ref, ssdn_ref, expand_ref,
                  wo_ref, o_ref, xext, uext, state, z_s, xs_s, bc_s, ee_s, y_s,
                  (i - MIX_NW) % tiles_per_seq)


def _mix_tile(x_ref, g_ref, wz_ref, wxbc_ref, wu_ref, convw_ref, convb_ref,
              dtb_ref, alog_ref, dskip_ref, ssdn_ref, expand_ref,
              wo_ref, o_ref, xext, uext, state, z_s, xs_s, bc_s, ee_s, y_s, sj):
    ts = MIX_TS

    @pl.when(sj == 0)
    def _():
        xext[0:CONV_HIST, :] = jnp.zeros((CONV_HIST, CONV_DIM), F32)
        uext[0:POOL_HIST, :] = jnp.zeros((POOL_HIST, POOL_DIM), F32)
        state[...] = jnp.zeros_like(state)

    x = x_ref[...]
    h = _rms(x, g_ref[...]).astype(BF16)
    xext[CONV_HIST:CONV_HIST + ts, :] = _dot(h, wxbc_ref[...])
    u_dt = _dot(h, wu_ref[...])
    uext[POOL_HIST:POOL_HIST + ts, :] = u_dt[:, :POOL_DIM]

    dt_raw = u_dt[:, POOL_DIM:] + dtb_ref[...]
    dt = jnp.maximum(dt_raw, 0.0) + jnp.log1p(jnp.exp(-jnp.abs(dt_raw)))
    cs = dt * (-jnp.exp(alog_ref[...]))
    row_in_chunk = lax.broadcasted_iota(jnp.int32, (ts, LANES), 0) % CHUNK
    shift = 1
    while shift < CHUNK:
        cs = cs + jnp.where(row_in_chunk >= shift, pltpu.roll(cs, shift, axis=0), 0.0)
        shift *= 2

    slot = lax.broadcasted_iota(jnp.int32, (ts, LANES), 1) // HEAD_SLOT
    terms = _split3(dt) + _split3(cs)
    packed = jnp.zeros((ts, LANES), F32)
    for k, term in enumerate(terms):
        packed = jnp.where(slot == k, term, packed)
    ee_s[...] = _dot(packed.astype(BF16), expand_ref[...])

    ext = xext[...]
    ext1 = pltpu.roll(ext, 1, axis=0)
    pair = convw_ref[1:2, :] * ext + convw_ref[0:1, :] * ext1
    conv = (convb_ref[...] + convw_ref[3:4, :] * ext + convw_ref[2:3, :] * ext1
            + pltpu.roll(pair, 2, axis=0))
    conv = _silu(conv[CONV_HIST:, :])
    xs_s[...] = conv[:, :SSD_DIM]
    bc_s[...] = conv[:, SSD_DIM:]
    xext[0:CONV_HIST, :] = xext[ts:ts + CONV_HIST, :]

    t_glob = sj * ts + lax.broadcasted_iota(jnp.int32, (ts, 1), 0)
    pooled = []

    def pool_group(gi):
        win = POOL_WINDOWS[gi]
        pcols = slice(gi * POOL_GROUP_DIM, (gi + 1) * POOL_GROUP_DIM)
        acc = uext[:, pcols]
        u0 = acc[POOL_HIST:, :]
        span = 1
        while span < win:
            acc = acc + pltpu.roll(acc, span, axis=0)
            span *= 2
        acc = acc[POOL_HIST:, :]
        inv_cnt = 1.0 / jnp.minimum(t_glob + 1, win).astype(F32)
        pooled.append((acc * inv_cnt - u0).astype(BF16))

    sub = lax.broadcasted_iota(jnp.int32, (CHUNK, LANES), 0)
    lane = lax.broadcasted_iota(jnp.int32, (CHUNK, LANES), 1)
    diag = sub == lane % CHUNK
    causal = sub >= lane % CHUNK
    left = lane < CHUNK
    for c in range(ts // CHUNK):
        rows = slice(c * CHUNK, (c + 1) * CHUNK)
        b_all = bc_s[rows, 0:BC_DIM]
        cb_lhs = bc_s[rows, BC_DIM:2 * BC_DIM].astype(BF16)
        zero_b = jnp.zeros((2 * CHUNK, SSD_STATE), BF16)
        cb_rhs = []
        for g in range(SSD_NGROUPS):
            b_b = b_all[:, g * SSD_STATE:(g + 1) * SSD_STATE].astype(BF16)
            b2 = jnp.concatenate([b_b, b_b], axis=0)
            cb_rhs.append(jnp.concatenate(
                [b2 if k == g else zero_b for k in range(SSD_NGROUPS)], axis=-1))
        cb_all = lax.dot_general(cb_lhs, jnp.concatenate(cb_rhs, axis=0),
                                 (((1,), (1,)), ((), ())), preferred_element_type=F32)
        for g in range(SSD_NGROUPS):
            gcols = slice(g * GROUP_DIM, (g + 1) * GROUP_DIM)
            xg = xs_s[rows, gcols]
            b_mat = b_all[:, g * SSD_STATE:(g + 1) * SSD_STATE]
            c_b = cb_lhs[:, g * SSD_STATE:(g + 1) * SSD_STATE]
            cb2 = cb_all[:, g * 2 * CHUNK:(g + 1) * 2 * CHUNK]
            dt_e = ee_s[rows, gcols]
            cs_e = ee_s[rows, SSD_DIM + g * GROUP_DIM:SSD_DIM + (g + 1) * GROUP_DIM]
            cs_last = cs_e[CHUNK - 1:CHUNK, :]
            xdt = xg * dt_e
            xdt_b = xdt.astype(BF16)
            x_decayed = (xdt * jnp.exp(cs_last - cs_e)).astype(BF16)
            s_prev = state[g]
            y_off = _dot(c_b, s_prev.astype(BF16)) * jnp.exp(cs_e)
            s_new = _dot(b_mat.T.astype(BF16), x_decayed)
            state[g] = jnp.exp(cs_last) * s_prev + s_new
            for quad in range(GROUP_DIM // (2 * LANES)):
                scores, blocks = [], []
                for half in range(2):
                    p = 2 * quad + half
                    pc = slice(p * LANES, (p + 1) * LANES)
                    col = cs_e[:, pc]
                    rowv = jnp.sum(jnp.where(diag, col, 0.0), axis=0, keepdims=True)
                    decay = jnp.exp(jnp.where(causal, col - rowv, -jnp.inf))
                    scores.append((cb2 * decay).astype(BF16))
                    xp = xdt_b[:, pc]
                    zero = jnp.zeros_like(xp)
                    for blk in (jnp.where(left, xp, zero), jnp.where(left, zero, xp)):
                        blocks.append(jnp.concatenate([blk, zero] if half == 0 else [zero, blk], axis=-1))
                qc = slice(2 * quad * LANES, 2 * (quad + 1) * LANES)
                oc = slice(g * GROUP_DIM + qc.start, g * GROUP_DIM + qc.stop)
                y_s[rows, oc] = (_dot(jnp.concatenate(scores, axis=-1), jnp.concatenate(blocks, axis=0))
                                 + y_off[:, qc] + dskip_ref[:, oc] * xg[:, qc])
        if c % 2 == 1:
            pool_group(c // 2)

    z_s[...] = _dot(h, wz_ref[...])
    yg = y_s[...] * _silu(z_s[...])
    normed = []
    for g in range(SSD_NGROUPS):
        seg = yg[:, g * GROUP_DIM:(g + 1) * GROUP_DIM]
        normed.append(seg * lax.rsqrt(jnp.mean(seg * seg, axis=-1, keepdims=True) + NORM_EPS))
    y_ssd = (jnp.concatenate(normed, axis=-1) * ssdn_ref[...]).astype(BF16)

    y_pool = jnp.concatenate(pooled, axis=-1)
    uext[0:POOL_HIST, :] = uext[ts:ts + POOL_HIST, :]

    o_ref[...] = x + _dot(jnp.concatenate([y_ssd, y_pool], axis=-1), wo_ref[...])


def _expand_matrix():
    e = np.zeros((LANES, 2 * SSD_DIM), np.float32)
    for k in range(6):
        for hd in range(SSD_HEADS):
            base = (k // 3) * SSD_DIM + hd * SSD_HEAD_DIM
            e[k * HEAD_SLOT + hd, base:base + SSD_HEAD_DIM] = 1.0
    return e


def _mix(x2d, seq_len, gain, w_in, w_u, w_out, pool_w, conv_w, conv_b, dt_bias, a_log, d_skip_e,
         ssd_norm, pool_scale):
    m, d = x2d.shape
    ts = MIX_TS
    tiles_per_seq = seq_len // ts
    expand = jnp.asarray(_expand_matrix(), BF16)
    small = (conv_w, conv_b, dt_bias, a_log, d_skip_e, ssd_norm, expand, pool_scale)
    tile = pl.BlockSpec((ts, d), lambda i: (jnp.maximum(i - MIX_NW, 0), 0))
    n_u = POOL_DIM // MIX_WCOLS
    n_o = (SSD_DIM + POOL_DIM) // MIX_WCOLS
    n_p = len(POOL_WINDOWS)
    in_specs = [
        tile,
        _resident(gain.shape),
        pl.BlockSpec((MIX_WCOLS, d), lambda i: (jnp.minimum(i, MIX_NW - 1), 0)),
        pl.BlockSpec((MIX_WCOLS, d), lambda i: (jnp.minimum(i, n_u - 1), 0)),
        pl.BlockSpec((MIX_WCOLS, d), lambda i: (jnp.minimum(i, n_o - 1), 0)),
        pl.BlockSpec((1, POOL_GROUP_DIM, POOL_GROUP_DIM), lambda i: (jnp.minimum(i, n_p - 1), 0, 0)),
    ]
    in_specs += [_resident(w.shape) for w in small]
    return pl.pallas_call(
        functools.partial(_mix_kernel, tiles_per_seq=tiles_per_seq),
        out_shape=jax.ShapeDtypeStruct((m, d), F32),
        grid=(MIX_NW + m // ts,),
        in_specs=in_specs,
        out_specs=tile,
        scratch_shapes=[
            pltpu.VMEM((d, SSD_DIM), BF16),
            pltpu.VMEM((d, CONV_DIM), BF16),
            pltpu.VMEM((d, POOL_DIM + LANES), BF16),
            pltpu.VMEM((SSD_DIM + POOL_DIM, d), BF16),
            pltpu.VMEM((CONV_HIST + ts, CONV_DIM), F32),
            pltpu.VMEM((POOL_HIST + ts, POOL_DIM), F32),
            pltpu.VMEM((SSD_NGROUPS, SSD_STATE, GROUP_DIM), F32),
            pltpu.VMEM((ts, SSD_DIM), F32),
            pltpu.VMEM((ts, SSD_DIM), F32),
            pltpu.VMEM((ts, 2 * BC_DIM), F32),
            pltpu.VMEM((ts, 2 * SSD_DIM), F32),
            pltpu.VMEM((ts, SSD_DIM), F32),
        ],
        compiler_params=pltpu.CompilerParams(
            dimension_semantics=("arbitrary",), vmem_limit_bytes=VMEM_LIMIT),
        name="mix",
    )(x2d, gain, w_in, w_u, w_out, pool_w, *small)


def kernel(x, ffn1_norm, ffn1_w_gate, ffn1_w_up, ffn1_w_down, mix_norm, w_in, conv_w, conv_b,
           dt_bias, a_log, d_skip, ssd_norm, pool_w, pool_scale, w_out, ffn2_norm,
           ffn2_w_gate, ffn2_w_up, ffn2_w_down, final_norm):
    b, s, d = x.shape
    assert ffn1_norm.shape[0] == 1, "one macaron block"
    o3 = SSD_DIM + CONV_DIM + SSD_HEADS
    assert MIX_NW * MIX_WCOLS >= o3 and (MIX_NW - 1) * MIX_WCOLS == o3 - SSD_HEADS
    rep = LANES // SSD_HEADS
    row = lambda v: v.reshape(1, -1)
    ones = jnp.ones((1, d), F32)
    x2d = _ffn(x.reshape(b * s, d), row(ffn1_norm[0]), ffn1_w_gate[0], ffn1_w_up[0],
               ffn1_w_down[0], ones, final_norm=False, name="ffn1")
    x2d = _mix(
        x2d, s, row(mix_norm[0]), w_in[0].T, w_in[0].T[o3:], w_out[0], pool_w[0],
        conv_w[0], row(conv_b[0]),
        jnp.tile(row(dt_bias[0]), (1, rep)), jnp.tile(row(a_log[0]), (1, rep)),
        row(jnp.repeat(d_skip[0], SSD_HEAD_DIM)), row(ssd_norm[0]), row(pool_scale[0]))
    out = _ffn(x2d, row(ffn2_norm[0]), ffn2_w_gate[0], ffn2_w_up[0], ffn2_w_down[0],
               row(final_norm), final_norm=True, name="ffn2")
    return out.reshape(b, s, d)
```

```python
import functools

import numpy as np
import jax
import jax.numpy as jnp
from jax import lax
from jax.experimental import pallas as pl
from jax.experimental.pallas import tpu as pltpu

D_MODEL = 1024
CHUNK = 64
SSD_DIM = 1024
SSD_HEAD_DIM = 64
SSD_HEADS = 16
SSD_NGROUPS = 2
SSD_STATE = 128
SSD_CONV = 4
POOL_DIM = 1024
POOL_WINDOWS = (2, 4, 8, 16)
POOL_GROUP_DIM = 256
CONV_DIM = SSD_DIM + 2 * SSD_NGROUPS * SSD_STATE
D_FF = 2816
NORM_EPS = 1e-6

GROUP_DIM = SSD_DIM // SSD_NGROUPS
BC_DIM = SSD_NGROUPS * SSD_STATE
LANES = 128
SUBLANES = 8
HEAD_SLOT = 16
CONV_HIST = SUBLANES
POOL_HIST = 16

FFN_TM = 1024
FFN_TF = 256
FFN_NW = D_FF // FFN_TF
MIX_TS = 512
MIX_WCOLS = 256
MIX_NW = 11
VMEM_LIMIT = 56 * 1024 * 1024

F32 = jnp.float32
BF16 = jnp.bfloat16


def _silu(v):
    return v * jax.nn.sigmoid(v)


def _rms(v, gain):
    return v * lax.rsqrt(jnp.mean(v * v, axis=-1, keepdims=True) + NORM_EPS) * gain


def _dot(a, b):
    return jnp.dot(a, b, preferred_element_type=F32)


def _ffn_kernel(x_ref, g_ref, wg_ref, wu_ref, wd_ref, fg_ref, o_ref,
                wg_s, wu_s, wd_s, act_ref, *, final_norm):
    i = pl.program_id(0)

    @pl.when(i < FFN_NW)
    def _():
        wg_s[i] = wg_ref[...].astype(BF16)
        wu_s[i] = wu_ref[...].astype(BF16)
        wd_s[pl.ds(pl.multiple_of(i * FFN_TF, FFN_TF), FFN_TF), :] = wd_ref[...].astype(BF16)

    @pl.when(i >= FFN_NW)
    def _():
        x = x_ref[...]
        h = _rms(x, g_ref[...]).astype(BF16)
        for j in range(FFN_NW):
            gate = _dot(h, wg_s[j])
            up = _dot(h, wu_s[j])
            act_ref[:, j * FFN_TF:(j + 1) * FFN_TF] = (_silu(gate) * up).astype(BF16)
        y = x + 0.5 * _dot(act_ref[...], wd_s[...])
        if final_norm:
            y = _rms(y, fg_ref[...])
        o_ref[...] = y


def _resident(shape):
    return pl.BlockSpec(shape, lambda *_: (0,) * len(shape), pipeline_mode=pl.Buffered(1))


def _ffn(x2d, gain, w_gate, w_up, w_down, final_gain, *, final_norm, name):
    m = x2d.shape[0]
    wchunk = lambda i: jnp.minimum(i, FFN_NW - 1)
    tile = pl.BlockSpec((FFN_TM, D_MODEL), lambda i: (jnp.maximum(i - FFN_NW, 0), 0))
    return pl.pallas_call(
        functools.partial(_ffn_kernel, final_norm=final_norm),
        out_shape=jax.ShapeDtypeStruct((m, D_MODEL), F32),
        grid=(FFN_NW + m // FFN_TM,),
        in_specs=[
            tile,
            _resident((1, D_MODEL)),
            pl.BlockSpec((D_MODEL, FFN_TF), lambda i: (0, wchunk(i))),
            pl.BlockSpec((D_MODEL, FFN_TF), lambda i: (0, wchunk(i))),
            pl.BlockSpec((FFN_TF, D_MODEL), lambda i: (wchunk(i), 0)),
            _resident((1, D_MODEL)),
        ],
        out_specs=tile,
        scratch_shapes=[
            pltpu.VMEM((FFN_NW, D_MODEL, FFN_TF), BF16),
            pltpu.VMEM((FFN_NW, D_MODEL, FFN_TF), BF16),
            pltpu.VMEM((D_FF, D_MODEL), BF16),
            pltpu.VMEM((FFN_TM, D_FF), BF16),
        ],
        compiler_params=pltpu.CompilerParams(
            dimension_semantics=("arbitrary",), vmem_limit_bytes=VMEM_LIMIT),
        name=name,
    )(x2d, gain, w_gate, w_up, w_down, final_gain)


def _split3(v):
    hi = v.astype(BF16).astype(F32)
    r = v - hi
    mid = r.astype(BF16).astype(F32)
    return hi, mid, r - mid


def _mix_kernel(x_ref, g_ref, win_ref, wuf_ref, wof_ref, poolf_ref, convw_ref, convb_ref,
                dtb_ref, alog_ref, dskip_ref, ssdn_ref, expand_ref, pools_ref, o_ref,
                wz_ref, wxbc_ref, wu_ref, wo_ref,
                xext, uext, state, z_s, xs_s, bc_s, ee_s, y_s, *, tiles_per_seq):
    i = pl.program_id(0)
    z_blocks = SSD_DIM // MIX_WCOLS
    xbc_blocks = CONV_DIM // MIX_WCOLS

    for j in range(MIX_NW):
        @pl.when(i == j)
        def _(j=j):
            blk = win_ref[...].T
            if j < z_blocks:
                wz_ref[:, j * MIX_WCOLS:(j + 1) * MIX_WCOLS] = blk.astype(BF16)
            elif j < z_blocks + xbc_blocks:
                k = j - z_blocks
                wxbc_ref[:, k * MIX_WCOLS:(k + 1) * MIX_WCOLS] = blk.astype(BF16)
            else:
                lane = lax.broadcasted_iota(jnp.int32, (D_MODEL, LANES), 1)
                rep = jnp.where(lane < SSD_HEADS, blk[:, :LANES], 0.0)
                width = SSD_HEADS
                while width < LANES:
                    rep = rep + pltpu.roll(rep, width, axis=1)
                    width *= 2
                wu_ref[:, POOL_DIM:] = rep.astype(BF16)
            if j < len(POOL_WINDOWS):
                gcols = slice(j * POOL_GROUP_DIM, (j + 1) * POOL_GROUP_DIM)
                folded = _dot(wuf_ref[...].T.astype(BF16), poolf_ref[0].astype(BF16))
                wu_ref[:, gcols] = (folded * pools_ref[:, gcols]).astype(BF16)
            if j < (SSD_DIM + POOL_DIM) // MIX_WCOLS:
                wo_ref[j * MIX_WCOLS:(j + 1) * MIX_WCOLS, :] = wof_ref[...].astype(BF16)

    @pl.when(i >= MIX_NW)
    def _():
        _mix_tile(x_ref, g_ref, wz_ref, wxbc_ref, wu_ref, convw_ref, convb_ref,
                  dtb_ref, alog_ref, dskip_ref, ssdn_ref, expand_ref,
                  wo_ref, o_ref, xext, uext, state, z_s, xs_s, bc_s, ee_s, y_s,
                  (i - MIX_NW) % tiles_per_seq)


def _mix_tile(x_ref, g_ref, wz_ref, wxbc_ref, wu_ref, convw_ref, convb_ref,
              dtb_ref, alog_ref, dskip_ref, ssdn_ref, expand_ref,
              wo_ref, o_ref, xext, uext, state, z_s, xs_s, bc_s, ee_s, y_s, sj):
    ts = MIX_TS

    @pl.when(sj == 0)
    def _():
        xext[0:CONV_HIST, :] = jnp.zeros((CONV_HIST, CONV_DIM), F32)
        uext[0:POOL_HIST, :] = jnp.zeros((POOL_HIST, POOL_DIM), F32)
        state[...] = jnp.zeros_like(state)

    x = x_ref[...]
    h = _rms(x, g_ref[...]).astype(BF16)
    xext[CONV_HIST:CONV_HIST + ts, :] = _dot(h, wxbc_ref[...])
    u_dt = _dot(h, wu_ref[...])
    uext[POOL_HIST:POOL_HIST + ts, :] = u_dt[:, :POOL_DIM]

    dt_raw = u_dt[:, POOL_DIM:] + dtb_ref[...]
    dt = jnp.maximum(dt_raw, 0.0) + jnp.log1p(jnp.exp(-jnp.abs(dt_raw)))
    cs = dt * (-jnp.exp(alog_ref[...]))
    row_in_chunk = lax.broadcasted_iota(jnp.int32, (ts, LANES), 0) % CHUNK
    shift = 1
    while shift < CHUNK:
        cs = cs + jnp.where(row_in_chunk >= shift, pltpu.roll(cs, shift, axis=0), 0.0)
        shift *= 2

    slot = lax.broadcasted_iota(jnp.int32, (ts, LANES), 1) // HEAD_SLOT
    terms = _split3(dt) + _split3(cs)
    packed = jnp.zeros((ts, LANES), F32)
    for k, term in enumerate(terms):
        packed = jnp.where(slot == k, term, packed)
    ee_s[...] = _dot(packed.astype(BF16), expand_ref[...])

    ext = xext[...]
    ext1 = pltpu.roll(ext, 1, axis=0)
    pair = convw_ref[1:2, :] * ext + convw_ref[0:1, :] * ext1
    conv = (convb_ref[...] + convw_ref[3:4, :] * ext + convw_ref[2:3, :] * ext1
            + pltpu.roll(pair, 2, axis=0))
    conv = _silu(conv[CONV_HIST:, :])
    xs_s[...] = conv[:, :SSD_DIM]
    bc_s[...] = conv[:, SSD_DIM:]
    xext[0:CONV_HIST, :] = xext[ts:ts + CONV_HIST, :]

    t_glob = sj * ts + lax.broadcasted_iota(jnp.int32, (ts, 1), 0)
    pooled = []

    def pool_group(gi):
        win = POOL_WINDOWS[gi]
        pcols = slice(gi * POOL_GROUP_DIM, (gi + 1) * POOL_GROUP_DIM)
        acc = uext[:, pcols]
        u0 = acc[POOL_HIST:, :]
        span = 1
        while span < win:
            acc = acc + pltpu.roll(acc, span, axis=0)
            span *= 2
        acc = acc[POOL_HIST:, :]
        inv_cnt = 1.0 / jnp.minimum(t_glob + 1, win).astype(F32)
        pooled.append((acc * inv_cnt - u0).astype(BF16))

    sub = lax.broadcasted_iota(jnp.int32, (CHUNK, LANES), 0)
    lane = lax.broadcasted_iota(jnp.int32, (CHUNK, LANES), 1)
    diag = sub == lane % CHUNK
    causal = sub >= lane % CHUNK
    left = lane < CHUNK
    for c in range(ts // CHUNK):
        rows = slice(c * CHUNK, (c + 1) * CHUNK)
        b_all = bc_s[rows, 0:BC_DIM]
        cb_lhs = bc_s[rows, BC_DIM:2 * BC_DIM].astype(BF16)
        zero_b = jnp.zeros((2 * CHUNK, SSD_STATE), BF16)
        cb_rhs = []
        for g in range(SSD_NGROUPS):
            b_b = b_all[:, g * SSD_STATE:(g + 1) * SSD_STATE].astype(BF16)
            b2 = jnp.concatenate([b_b, b_b], axis=0)
            cb_rhs.append(jnp.concatenate(
                [b2 if k == g else zero_b for k in range(SSD_NGROUPS)], axis=-1))
        cb_all = lax.dot_general(cb_lhs, jnp.concatenate(cb_rhs, axis=0),
                                 (((1,), (1,)), ((), ())), preferred_element_type=F32)
        for g in range(SSD_NGROUPS):
            gcols = slice(g * GROUP_DIM, (g + 1) * GROUP_DIM)
            xg = xs_s[rows, gcols]
            b_mat = b_all[:, g * SSD_STATE:(g + 1) * SSD_STATE]
            c_b = cb_lhs[:, g * SSD_STATE:(g + 1) * SSD_STATE]
            cb2 = cb_all[:, g * 2 * CHUNK:(g + 1) * 2 * CHUNK]
            dt_e = ee_s[rows, gcols]
            cs_e = ee_s[rows, SSD_DIM + g * GROUP_DIM:SSD_DIM + (g + 1) * GROUP_DIM]
            cs_last = cs_e[CHUNK - 1:CHUNK, :]
            xdt = xg * dt_e
            xdt_b = xdt.astype(BF16)
            x_decayed = (xdt * jnp.exp(cs_last - cs_e)).astype(BF16)
            s_prev = state[g]
            y_off = _dot(c_b, s_prev.astype(BF16)) * jnp.exp(cs_e)
            s_new = _dot(b_mat.T.astype(BF16), x_decayed)
            state[g] = jnp.exp(cs_last) * s_prev + s_new
            for quad in range(GROUP_DIM // (2 * LANES)):
                scores, blocks = [], []
                for half in range(2):
                    p = 2 * quad + half
                    pc = slice(p * LANES, (p + 1) * LANES)
                    col = cs_e[:, pc]
                    rowv = jnp.sum(jnp.where(diag, col, 0.0), axis=0, keepdims=True)
                    decay = jnp.exp(jnp.where(causal, col - rowv, -jnp.inf))
                    scores.append((cb2 * decay).astype(BF16))
                    xp = xdt_b[:, pc]
                    zero = jnp.zeros_like(xp)
                    for blk in (jnp.where(left, xp, zero), jnp.where(left, zero, xp)):
                        blocks.append(jnp.concatenate([blk, zero] if half == 0 else [zero, blk], axis=-1))
                qc = slice(2 * quad * LANES, 2 * (quad + 1) * LANES)
                oc = slice(g * GROUP_DIM + qc.start, g * GROUP_DIM + qc.stop)
                y_s[rows, oc] = (_dot(jnp.concatenate(scores, axis=-1), jnp.concatenate(blocks, axis=0))
                                 + y_off[:, qc] + dskip_ref[:, oc] * xg[:, qc])
        if c % 2 == 1:
            pool_group(c // 2)

    z_s[...] = _dot(h, wz_ref[...])
    yg = y_s[...] * _silu(z_s[...])
    normed = []
    for g in range(SSD_NGROUPS):
        seg = yg[:, g * GROUP_DIM:(g + 1) * GROUP_DIM]
        normed.append(seg * lax.rsqrt(jnp.mean(seg * seg, axis=-1, keepdims=True) + NORM_EPS))
    y_ssd = (jnp.concatenate(normed, axis=-1) * ssdn_ref[...]).astype(BF16)

    y_pool = jnp.concatenate(pooled, axis=-1)
    uext[0:POOL_HIST, :] = uext[ts:ts + POOL_HIST, :]

    o_ref[...] = x + _dot(jnp.concatenate([y_ssd, y_pool], axis=-1), wo_ref[...])


def _expand_matrix():
    e = np.zeros((LANES, 2 * SSD_DIM), np.float32)
    for k in range(6):
        for hd in range(SSD_HEADS):
            base = (k // 3) * SSD_DIM + hd * SSD_HEAD_DIM
            e[k * HEAD_SLOT + hd, base:base + SSD_HEAD_DIM] = 1.0
    return e


def _mix(x2d, seq_len, gain, w_in, u_row0, w_out, pool_w, conv_w, conv_b, dt_bias, a_log, d_skip_e,
         ssd_norm, pool_scale):
    m, d = x2d.shape
    ts = MIX_TS
    tiles_per_seq = seq_len // ts
    expand = jnp.asarray(_expand_matrix(), BF16)
    small = (conv_w, conv_b, dt_bias, a_log, d_skip_e, ssd_norm, expand, pool_scale)
    tile = pl.BlockSpec((ts, d), lambda i: (jnp.maximum(i - MIX_NW, 0), 0))
    n_u = POOL_DIM // MIX_WCOLS
    n_o = (SSD_DIM + POOL_DIM) // MIX_WCOLS
    n_p = len(POOL_WINDOWS)
    in_specs = [
        tile,
        _resident(gain.shape),
        pl.BlockSpec((MIX_WCOLS, d), lambda i: (jnp.minimum(i, MIX_NW - 1), 0)),
        pl.BlockSpec((pl.Element(MIX_WCOLS), pl.Element(d)),
                     lambda i: (pl.multiple_of(u_row0 + jnp.minimum(i, n_u - 1) * MIX_WCOLS, SUBLANES), 0)),
        pl.BlockSpec((MIX_WCOLS, d), lambda i: (jnp.minimum(i, n_o - 1), 0)),
        pl.BlockSpec((1, POOL_GROUP_DIM, POOL_GROUP_DIM), lambda i: (jnp.minimum(i, n_p - 1), 0, 0)),
    ]
    in_specs += [_resident(w.shape) for w in small]
    return pl.pallas_call(
        functools.partial(_mix_kernel, tiles_per_seq=tiles_per_seq),
        out_shape=jax.ShapeDtypeStruct((m, d), F32),
        grid=(MIX_NW + m // ts,),
        in_specs=in_specs,
        out_specs=tile,
        scratch_shapes=[
            pltpu.VMEM((d, SSD_DIM), BF16),
            pltpu.VMEM((d, CONV_DIM), BF16),
            pltpu.VMEM((d, POOL_DIM + LANES), BF16),
            pltpu.VMEM((SSD_DIM + POOL_DIM, d), BF16),
            pltpu.VMEM((CONV_HIST + ts, CONV_DIM), F32),
            pltpu.VMEM((POOL_HIST + ts, POOL_DIM), F32),
            pltpu.VMEM((SSD_NGROUPS, SSD_STATE, GROUP_DIM), F32),
            pltpu.VMEM((ts, SSD_DIM), F32),
            pltpu.VMEM((ts, SSD_DIM), F32),
            pltpu.VMEM((ts, 2 * BC_DIM), F32),
            pltpu.VMEM((ts, 2 * SSD_DIM), F32),
            pltpu.VMEM((ts, SSD_DIM), F32),
        ],
        compiler_params=pltpu.CompilerParams(
            dimension_semantics=("arbitrary",), vmem_limit_bytes=VMEM_LIMIT),
        name="mix",
    )(x2d, gain, w_in, w_in, w_out, pool_w, *small)


def kernel(x, ffn1_norm, ffn1_w_gate, ffn1_w_up, ffn1_w_down, mix_norm, w_in, conv_w, conv_b,
           dt_bias, a_log, d_skip, ssd_norm, pool_w, pool_scale, w_out, ffn2_norm,
           ffn2_w_gate, ffn2_w_up, ffn2_w_down, final_norm):
    b, s, d = x.shape
    assert ffn1_norm.shape[0] == 1, "one macaron block"
    o3 = SSD_DIM + CONV_DIM + SSD_HEADS
    assert MIX_NW * MIX_WCOLS >= o3 and (MIX_NW - 1) * MIX_WCOLS == o3 - SSD_HEADS
    rep = LANES // SSD_HEADS
    row = lambda v: v.reshape(1, -1)
    ones = jnp.ones((1, d), F32)
    x2d = _ffn(x.reshape(b * s, d), row(ffn1_norm[0]), ffn1_w_gate[0], ffn1_w_up[0],
               ffn1_w_down[0], ones, final_norm=False, name="ffn1")
    x2d = _mix(
        x2d, s, row(mix_norm[0]), w_in[0].T, o3, w_out[0], pool_w[0],
        conv_w[0], row(conv_b[0]),
        jnp.tile(row(dt_bias[0]), (1, rep)), jnp.tile(row(a_log[0]), (1, rep)),
        row(jnp.repeat(d_skip[0], SSD_HEAD_DIM)), row(ssd_norm[0]), row(pool_scale[0]))
    out = _ffn(x2d, row(ffn2_norm[0]), ffn2_w_gate[0], ffn2_w_up[0], ffn2_w_down[0],
               row(final_norm), final_norm=True, name="ffn2")
    return out.reshape(b, s, d)
```

```python
import functools

import numpy as np
import jax
import jax.numpy as jnp
from jax import lax
from jax.experimental import pallas as pl
from jax.experimental.pallas import tpu as pltpu

D_MODEL = 1024
CHUNK = 64
SSD_DIM = 1024
SSD_HEAD_DIM = 64
SSD_HEADS = 16
SSD_NGROUPS = 2
SSD_STATE = 128
SSD_CONV = 4
POOL_DIM = 1024
POOL_WINDOWS = (2, 4, 8, 16)
POOL_GROUP_DIM = 256
CONV_DIM = SSD_DIM + 2 * SSD_NGROUPS * SSD_STATE
D_FF = 2816
NORM_EPS = 1e-6

GROUP_DIM = SSD_DIM // SSD_NGROUPS
BC_DIM = SSD_NGROUPS * SSD_STATE
LANES = 128
SUBLANES = 8
HEAD_SLOT = 16
CONV_HIST = SUBLANES
POOL_HIST = 16

FFN_TM = 1024
FFN_TF = 256
FFN_NW = D_FF // FFN_TF
MIX_TS = 512
MIX_WCOLS = 256
MIX_NW = 11
VMEM_LIMIT = 56 * 1024 * 1024

F32 = jnp.float32
BF16 = jnp.bfloat16


def _silu(v):
    half = 0.5 * v
    return half + half * jnp.tanh(half)


def _rms(v, gain):
    return v * lax.rsqrt(jnp.mean(v * v, axis=-1, keepdims=True) + NORM_EPS) * gain


def _dot(a, b):
    return jnp.dot(a, b, preferred_element_type=F32)


def _ffn_kernel(x_ref, g_ref, wg_ref, wu_ref, wd_ref, fg_ref, o_ref,
                wg_s, wu_s, wd_s, act_ref, *, final_norm):
    i = pl.program_id(0)

    @pl.when(i < FFN_NW)
    def _():
        wg_s[i] = wg_ref[...].astype(BF16)
        wu_s[i] = wu_ref[...].astype(BF16)
        wd_s[pl.ds(pl.multiple_of(i * FFN_TF, FFN_TF), FFN_TF), :] = wd_ref[...].astype(BF16)

    @pl.when(i >= FFN_NW)
    def _():
        x = x_ref[...]
        h = _rms(x, g_ref[...]).astype(BF16)
        for j in range(FFN_NW):
            gate = _dot(h, wg_s[j])
            up = _dot(h, wu_s[j])
            act_ref[:, j * FFN_TF:(j + 1) * FFN_TF] = (_silu(gate) * up).astype(BF16)
        y = x + 0.5 * _dot(act_ref[...], wd_s[...])
        if final_norm:
            y = _rms(y, fg_ref[...])
        o_ref[...] = y


def _resident(shape):
    return pl.BlockSpec(shape, lambda *_: (0,) * len(shape), pipeline_mode=pl.Buffered(1))


def _ffn(x2d, gain, w_gate, w_up, w_down, final_gain, *, final_norm, name):
    m = x2d.shape[0]
    wchunk = lambda i: jnp.minimum(i, FFN_NW - 1)
    tile = pl.BlockSpec((FFN_TM, D_MODEL), lambda i: (jnp.maximum(i - FFN_NW, 0), 0))
    return pl.pallas_call(
        functools.partial(_ffn_kernel, final_norm=final_norm),
        out_shape=jax.ShapeDtypeStruct((m, D_MODEL), F32),
        grid=(FFN_NW + m // FFN_TM,),
        in_specs=[
            tile,
            _resident((1, D_MODEL)),
            pl.BlockSpec((D_MODEL, FFN_TF), lambda i: (0, wchunk(i))),
            pl.BlockSpec((D_MODEL, FFN_TF), lambda i: (0, wchunk(i))),
            pl.BlockSpec((FFN_TF, D_MODEL), lambda i: (wchunk(i), 0)),
            _resident((1, D_MODEL)),
        ],
        out_specs=tile,
        scratch_shapes=[
            pltpu.VMEM((FFN_NW, D_MODEL, FFN_TF), BF16),
            pltpu.VMEM((FFN_NW, D_MODEL, FFN_TF), BF16),
            pltpu.VMEM((D_FF, D_MODEL), BF16),
            pltpu.VMEM((FFN_TM, D_FF), BF16),
        ],
        compiler_params=pltpu.CompilerParams(
            dimension_semantics=("arbitrary",), vmem_limit_bytes=VMEM_LIMIT),
        name=name,
    )(x2d, gain, w_gate, w_up, w_down, final_gain)


def _split3(v):
    hi = v.astype(BF16).astype(F32)
    r = v - hi
    mid = r.astype(BF16).astype(F32)
    return hi, mid, r - mid


def _mix_kernel(x_ref, g_ref, win_ref, wuf_ref, wof_ref, poolf_ref, convw_ref, convb_ref,
                dtb_ref, alog_ref, dskip_ref, ssdn_ref, expand_ref, pools_ref, o_ref,
                wz_ref, wxbc_ref, wu_ref, wo_ref,
                xext, uext, state, z_s, xs_s, bc_s, ee_s, y_s, *, tiles_per_seq):
    i = pl.program_id(0)
    z_blocks = SSD_DIM // MIX_WCOLS
    xbc_blocks = CONV_DIM // MIX_WCOLS

    for j in range(MIX_NW):
        @pl.when(i == j)
        def _(j=j):
            blk = win_ref[...].T
            if j < z_blocks:
                wz_ref[:, j * MIX_WCOLS:(j + 1) * MIX_WCOLS] = blk.astype(BF16)
            elif j < z_blocks + xbc_blocks:
                k = j - z_blocks
                wxbc_ref[:, k * MIX_WCOLS:(k + 1) * MIX_WCOLS] = blk.astype(BF16)
            else:
                lane = lax.broadcasted_iota(jnp.int32, (D_MODEL, LANES), 1)
                rep = jnp.where(lane < SSD_HEADS, blk[:, :LANES], 0.0)
                width = SSD_HEADS
                while width < LANES:
                    rep = rep + pltpu.roll(rep, width, axis=1)
                    width *= 2
                wu_ref[:, POOL_DIM:] = rep.astype(BF16)
            if j < len(POOL_WINDOWS):
                gcols = slice(j * POOL_GROUP_DIM, (j + 1) * POOL_GROUP_DIM)
                folded = _dot(wuf_ref[...].T.astype(BF16), poolf_ref[0].astype(BF16))
                wu_ref[:, gcols] = (folded * pools_ref[:, gcols]).astype(BF16)
            if j < (SSD_DIM + POOL_DIM) // MIX_WCOLS:
                wo_ref[j * MIX_WCOLS:(j + 1) * MIX_WCOLS, :] = wof_ref[...].astype(BF16)

    @pl.when(i >= MIX_NW)
    def _():
        _mix_tile(x_ref, g_ref, wz_ref, wxbc_ref, wu_ref, convw_ref, convb_ref,
                  dtb_ref, alog_ref, dskip_ref, ssdn_ref, expand_ref,
                  wo_ref, o_ref, xext, uext, state, z_s, xs_s, bc_s, ee_s, y_s,
                  (i - MIX_NW) % tiles_per_seq)


def _mix_tile(x_ref, g_ref, wz_ref, wxbc_ref, wu_ref, convw_ref, convb_ref,
              dtb_ref, alog_ref, dskip_ref, ssdn_ref, expand_ref,
              wo_ref, o_ref, xext, uext, state, z_s, xs_s, bc_s, ee_s, y_s, sj):
    ts = MIX_TS

    @pl.when(sj == 0)
    def _():
        xext[0:CONV_HIST, :] = jnp.zeros((CONV_HIST, CONV_DIM), F32)
        uext[0:POOL_HIST, :] = jnp.zeros((POOL_HIST, POOL_DIM), F32)
        state[...] = jnp.zeros_like(state)

    x = x_ref[...]
    h = _rms(x, g_ref[...]).astype(BF16)
    xext[CONV_HIST:CONV_HIST + ts, :] = _dot(h, wxbc_ref[...])
    u_dt = _dot(h, wu_ref[...])
    uext[POOL_HIST:POOL_HIST + ts, :] = u_dt[:, :POOL_DIM]

    dt_raw = u_dt[:, POOL_DIM:] + dtb_ref[...]
    dt = jnp.maximum(dt_raw, 0.0) + jnp.log1p(jnp.exp(-jnp.abs(dt_raw)))
    cs = dt * (-jnp.exp(alog_ref[...]))
    row_in_chunk = lax.broadcasted_iota(jnp.int32, (ts, LANES), 0) % CHUNK
    shift = 1
    while shift < CHUNK:
        cs = cs + jnp.where(row_in_chunk >= shift, pltpu.roll(cs, shift, axis=0), 0.0)
        shift *= 2

    slot = lax.broadcasted_iota(jnp.int32, (ts, LANES), 1) // HEAD_SLOT
    terms = _split3(dt) + _split3(cs)
    packed = jnp.zeros((ts, LANES), F32)
    for k, term in enumerate(terms):
        packed = jnp.where(slot == k, term, packed)
    ee_s[...] = _dot(packed.astype(BF16), expand_ref[...])

    ext = xext[...]
    ext1 = pltpu.roll(ext, 1, axis=0)
    pair = convw_ref[1:2, :] * ext + convw_ref[0:1, :] * ext1
    conv = (convb_ref[...] + convw_ref[3:4, :] * ext + convw_ref[2:3, :] * ext1
            + pltpu.roll(pair, 2, axis=0))
    conv = _silu(conv[CONV_HIST:, :])
    xs_s[...] = conv[:, :SSD_DIM]
    bc_s[...] = conv[:, SSD_DIM:]
    xext[0:CONV_HIST, :] = xext[ts:ts + CONV_HIST, :]

    t_glob = sj * ts + lax.broadcasted_iota(jnp.int32, (ts, 1), 0)
    pooled = []

    def pool_group(gi):
        win = POOL_WINDOWS[gi]
        pcols = slice(gi * POOL_GROUP_DIM, (gi + 1) * POOL_GROUP_DIM)
        acc = uext[:, pcols]
        u0 = acc[POOL_HIST:, :]
        span = 1
        while span < win:
            acc = acc + pltpu.roll(acc, span, axis=0)
            span *= 2
        acc = acc[POOL_HIST:, :]
        inv_cnt = 1.0 / jnp.minimum(t_glob + 1, win).astype(F32)
        pooled.append((acc * inv_cnt - u0).astype(BF16))

    sub = lax.broadcasted_iota(jnp.int32, (CHUNK, LANES), 0)
    lane = lax.broadcasted_iota(jnp.int32, (CHUNK, LANES), 1)
    causal = sub >= lane % CHUNK
    left = lane < CHUNK
    sub8 = lax.broadcasted_iota(jnp.int32, (SUBLANES, LANES), 0)
    lane8 = lax.broadcasted_iota(jnp.int32, (SUBLANES, LANES), 1) % CHUNK
    lane_group = lane8 // SUBLANES
    on_diag = sub8 == lane8 % SUBLANES

    def diag_row(col):
        merged = col[0:SUBLANES, :]
        for i in range(1, CHUNK // SUBLANES):
            merged = jnp.where(lane_group == i, col[i * SUBLANES:(i + 1) * SUBLANES, :], merged)
        return jnp.sum(jnp.where(on_diag, merged, 0.0), axis=0, keepdims=True)
    for c in range(ts // CHUNK):
        rows = slice(c * CHUNK, (c + 1) * CHUNK)
        b_all = bc_s[rows, 0:BC_DIM]
        cb_lhs = bc_s[rows, BC_DIM:2 * BC_DIM].astype(BF16)
        zero_b = jnp.zeros((2 * CHUNK, SSD_STATE), BF16)
        cb_rhs = []
        for g in range(SSD_NGROUPS):
            b_b = b_all[:, g * SSD_STATE:(g + 1) * SSD_STATE].astype(BF16)
            b2 = jnp.concatenate([b_b, b_b], axis=0)
            cb_rhs.append(jnp.concatenate(
                [b2 if k == g else zero_b for k in range(SSD_NGROUPS)], axis=-1))
        cb_all = lax.dot_general(cb_lhs, jnp.concatenate(cb_rhs, axis=0),
                                 (((1,), (1,)), ((), ())), preferred_element_type=F32)
        for g in range(SSD_NGROUPS):
            c_b = cb_lhs[:, g * SSD_STATE:(g + 1) * SSD_STATE]
            bt_b = b_all[:, g * SSD_STATE:(g + 1) * SSD_STATE].T.astype(BF16)
            cb2 = cb_all[:, g * 2 * CHUNK:(g + 1) * 2 * CHUNK]
            for quad in range(GROUP_DIM // (2 * LANES)):
                qc = slice(2 * quad * LANES, 2 * (quad + 1) * LANES)
                oc = slice(g * GROUP_DIM + qc.start, g * GROUP_DIM + qc.stop)
                xq = xs_s[rows, oc]
                cs_q = ee_s[rows, SSD_DIM + oc.start:SSD_DIM + oc.stop]
                cs_last = cs_q[CHUNK - 1:CHUNK, :]
                xdt = xq * ee_s[rows, oc]
                xdt_b = xdt.astype(BF16)
                x_decayed = (xdt * jnp.exp(cs_last - cs_q)).astype(BF16)
                s_prev = state[g, :, qc]
                y_off = _dot(c_b, s_prev.astype(BF16)) * jnp.exp(cs_q)
                state[g, :, qc] = jnp.exp(cs_last) * s_prev + _dot(bt_b, x_decayed)
                scores, blocks = [], []
                for half in range(2):
                    pc = slice(half * LANES, (half + 1) * LANES)
                    col = cs_q[:, pc]
                    rowv = diag_row(col)
                    decay = jnp.exp(jnp.where(causal, col - rowv, -jnp.inf))
                    scores.append((cb2 * decay).astype(BF16))
                    xp = xdt_b[:, pc]
                    zero = jnp.zeros_like(xp)
                    for blk in (jnp.where(left, xp, zero), jnp.where(left, zero, xp)):
                        blocks.append(jnp.concatenate([blk, zero] if half == 0 else [zero, blk], axis=-1))
                y_s[rows, oc] = (_dot(jnp.concatenate(scores, axis=-1), jnp.concatenate(blocks, axis=0))
                                 + y_off + dskip_ref[:, oc] * xq)
        if c % 2 == 1:
            pool_group(c // 2)

    z_s[...] = _dot(h, wz_ref[...])
    yg = y_s[...] * _silu(z_s[...])
    normed = []
    for g in range(SSD_NGROUPS):
        seg = yg[:, g * GROUP_DIM:(g + 1) * GROUP_DIM]
        normed.append(seg * lax.rsqrt(jnp.mean(seg * seg, axis=-1, keepdims=True) + NORM_EPS))
    y_ssd = (jnp.concatenate(normed, axis=-1) * ssdn_ref[...]).astype(BF16)

    y_pool = jnp.concatenate(pooled, axis=-1)
    uext[0:POOL_HIST, :] = uext[ts:ts + POOL_HIST, :]

    o_ref[...] = x + _dot(jnp.concatenate([y_ssd, y_pool], axis=-1), wo_ref[...])


def _expand_matrix():
    e = np.zeros((LANES, 2 * SSD_DIM), np.float32)
    for k in range(6):
        for hd in range(SSD_HEADS):
            base = (k // 3) * SSD_DIM + hd * SSD_HEAD_DIM
            e[k * HEAD_SLOT + hd, base:base + SSD_HEAD_DIM] = 1.0
    return e


def _mix(x2d, seq_len, gain, w_in, u_row0, w_out, pool_w, conv_w, conv_b, dt_bias, a_log, d_skip_e,
         ssd_norm, pool_scale):
    m, d = x2d.shape
    ts = MIX_TS
    tiles_per_seq = seq_len // ts
    expand = jnp.asarray(_expand_matrix(), BF16)
    small = (conv_w, conv_b, dt_bias, a_log, d_skip_e, ssd_norm, expand, pool_scale)
    tile = pl.BlockSpec((ts, d), lambda i: (jnp.maximum(i - MIX_NW, 0), 0))
    n_u = POOL_DIM // MIX_WCOLS
    n_o = (SSD_DIM + POOL_DIM) // MIX_WCOLS
    n_p = len(POOL_WINDOWS)
    in_specs = [
        tile,
        _resident(gain.shape),
        pl.BlockSpec((MIX_WCOLS, d), lambda i: (jnp.minimum(i, MIX_NW - 1), 0)),
        pl.BlockSpec((pl.Element(MIX_WCOLS), pl.Element(d)),
                     lambda i: (pl.multiple_of(u_row0 + jnp.minimum(i, n_u - 1) * MIX_WCOLS, SUBLANES), 0)),
        pl.BlockSpec((MIX_WCOLS, d), lambda i: (jnp.minimum(i, n_o - 1), 0)),
        pl.BlockSpec((1, POOL_GROUP_DIM, POOL_GROUP_DIM), lambda i: (jnp.minimum(i, n_p - 1), 0, 0)),
    ]
    in_specs += [_resident(w.shape) for w in small]
    return pl.pallas_call(
        functools.partial(_mix_kernel, tiles_per_seq=tiles_per_seq),
        out_shape=jax.ShapeDtypeStruct((m, d), F32),
        grid=(MIX_NW + m // ts,),
        in_specs=in_specs,
        out_specs=tile,
        scratch_shapes=[
            pltpu.VMEM((d, SSD_DIM), BF16),
            pltpu.VMEM((d, CONV_DIM), BF16),
            pltpu.VMEM((d, POOL_DIM + LANES), BF16),
            pltpu.VMEM((SSD_DIM + POOL_DIM, d), BF16),
            pltpu.VMEM((CONV_HIST + ts, CONV_DIM), F32),
            pltpu.VMEM((POOL_HIST + ts, POOL_DIM), F32),
            pltpu.VMEM((SSD_NGROUPS, SSD_STATE, GROUP_DIM), F32),
            pltpu.VMEM((ts, SSD_DIM), F32),
            pltpu.VMEM((ts, SSD_DIM), F32),
            pltpu.VMEM((ts, 2 * BC_DIM), F32),
            pltpu.VMEM((ts, 2 * SSD_DIM), F32),
            pltpu.VMEM((ts, SSD_DIM), F32),
        ],
        compiler_params=pltpu.CompilerParams(
            dimension_semantics=("arbitrary",), vmem_limit_bytes=VMEM_LIMIT),
        name="mix",
    )(x2d, gain, w_in, w_in, w_out, pool_w, *small)


def kernel(x, ffn1_norm, ffn1_w_gate, ffn1_w_up, ffn1_w_down, mix_norm, w_in, conv_w, conv_b,
           dt_bias, a_log, d_skip, ssd_norm, pool_w, pool_scale, w_out, ffn2_norm,
           ffn2_w_gate, ffn2_w_up, ffn2_w_down, final_norm):
    b, s, d = x.shape
    assert ffn1_norm.shape[0] == 1, "one macaron block"
    o3 = SSD_DIM + CONV_DIM + SSD_HEADS
    assert MIX_NW * MIX_WCOLS >= o3 and (MIX_NW - 1) * MIX_WCOLS == o3 - SSD_HEADS
    rep = LANES // SSD_HEADS
    row = lambda v: v.reshape(1, -1)
    ones = jnp.ones((1, d), F32)
    x2d = _ffn(x.reshape(b * s, d), row(ffn1_norm[0]), ffn1_w_gate[0], ffn1_w_up[0],
               ffn1_w_down[0], ones, final_norm=False, name="ffn1")
    x2d = _mix(
        x2d, s, row(mix_norm[0]), w_in[0].T, o3, w_out[0], pool_w[0],
        conv_w[0], row(conv_b[0]),
        jnp.tile(row(dt_bias[0]), (1, rep)), jnp.tile(row(a_log[0]), (1, rep)),
        row(jnp.repeat(d_skip[0], SSD_HEAD_DIM)), row(ssd_norm[0]), row(pool_scale[0]))
    out = _ffn(x2d, row(ffn2_norm[0]), ffn2_w_gate[0], ffn2_w_up[0], ffn2_w_down[0],
               row(final_norm), final_norm=True, name="ffn2")
    return out.reshape(b, s, d)
```

```python
import functools

import numpy as np
import jax
import jax.numpy as jnp
from jax import lax
from jax.experimental import pallas as pl
from jax.experimental.pallas import tpu as pltpu

D_MODEL = 1024
CHUNK = 64
SSD_DIM = 1024
SSD_HEAD_DIM = 64
SSD_HEADS = 16
SSD_NGROUPS = 2
SSD_STATE = 128
SSD_CONV = 4
POOL_DIM = 1024
POOL_WINDOWS = (2, 4, 8, 16)
POOL_GROUP_DIM = 256
CONV_DIM = SSD_DIM + 2 * SSD_NGROUPS * SSD_STATE
D_FF = 2816
NORM_EPS = 1e-6

GROUP_DIM = SSD_DIM // SSD_NGROUPS
BC_DIM = SSD_NGROUPS * SSD_STATE
LANES = 128
SUBLANES = 8
HEAD_SLOT = 16
CONV_HIST = SUBLANES
POOL_HIST = 16

FFN_TM = 1024
FFN_TF = 256
FFN_NW = D_FF // FFN_TF
MIX_TS = 512
MIX_WCOLS = 256
MIX_NW = 11
VMEM_LIMIT = 56 * 1024 * 1024

F32 = jnp.float32
BF16 = jnp.bfloat16


def _silu(v):
    half = 0.5 * v
    return half + half * jnp.tanh(half)


def _rms(v, gain):
    return v * lax.rsqrt(jnp.mean(v * v, axis=-1, keepdims=True) + NORM_EPS) * gain


def _dot(a, b):
    return jnp.dot(a, b, preferred_element_type=F32)


def _ffn_kernel(x_ref, g_ref, wg_ref, wu_ref, wd_ref, fg_ref, o_ref,
                wg_s, wu_s, wd_s, act_ref, *, final_norm):
    i = pl.program_id(0)

    @pl.when(i < FFN_NW)
    def _():
        wg_s[i] = wg_ref[...].astype(BF16)
        wu_s[i] = wu_ref[...].astype(BF16)
        wd_s[pl.ds(pl.multiple_of(i * FFN_TF, FFN_TF), FFN_TF), :] = wd_ref[...].astype(BF16)

    @pl.when(i >= FFN_NW)
    def _():
        x = x_ref[...]
        h = _rms(x, g_ref[...]).astype(BF16)
        for j in range(FFN_NW):
            gate = _dot(h, wg_s[j])
            up = _dot(h, wu_s[j])
            act_ref[:, j * FFN_TF:(j + 1) * FFN_TF] = (_silu(gate) * up).astype(BF16)
        ssq = None
        for n in range(D_MODEL // FFN_TF):
            cols = slice(n * FFN_TF, (n + 1) * FFN_TF)
            yb = x[:, cols] + 0.5 * _dot(act_ref[...], wd_s[:, cols])
            o_ref[:, cols] = yb
            if final_norm:
                part = jnp.sum(yb * yb, axis=-1, keepdims=True)
                ssq = part if ssq is None else ssq + part
        if final_norm:
            scale = lax.rsqrt(ssq * (1.0 / D_MODEL) + NORM_EPS)
            o_ref[...] = o_ref[...] * scale * fg_ref[...]


def _resident(shape):
    return pl.BlockSpec(shape, lambda *_: (0,) * len(shape), pipeline_mode=pl.Buffered(1))


def _ffn(x2d, gain, w_gate, w_up, w_down, final_gain, *, final_norm, name):
    m = x2d.shape[0]
    wchunk = lambda i: jnp.minimum(i, FFN_NW - 1)
    tile = pl.BlockSpec((FFN_TM, D_MODEL), lambda i: (jnp.maximum(i - FFN_NW, 0), 0))
    return pl.pallas_call(
        functools.partial(_ffn_kernel, final_norm=final_norm),
        out_shape=jax.ShapeDtypeStruct((m, D_MODEL), F32),
        grid=(FFN_NW + m // FFN_TM,),
        in_specs=[
            tile,
            _resident((1, D_MODEL)),
            pl.BlockSpec((D_MODEL, FFN_TF), lambda i: (0, wchunk(i))),
            pl.BlockSpec((D_MODEL, FFN_TF), lambda i: (0, wchunk(i))),
            pl.BlockSpec((FFN_TF, D_MODEL), lambda i: (wchunk(i), 0)),
            _resident((1, D_MODEL)),
        ],
        out_specs=tile,
        scratch_shapes=[
            pltpu.VMEM((FFN_NW, D_MODEL, FFN_TF), BF16),
            pltpu.VMEM((FFN_NW, D_MODEL, FFN_TF), BF16),
            pltpu.VMEM((D_FF, D_MODEL), BF16),
            pltpu.VMEM((FFN_TM, D_FF), BF16),
        ],
        compiler_params=pltpu.CompilerParams(
            dimension_semantics=("arbitrary",), vmem_limit_bytes=VMEM_LIMIT),
        name=name,
    )(x2d, gain, w_gate, w_up, w_down, final_gain)


def _split3(v):
    hi = v.astype(BF16).astype(F32)
    r = v - hi
    mid = r.astype(BF16).astype(F32)
    return hi, mid, r - mid


def _mix_kernel(x_ref, g_ref, win_ref, wuf_ref, wof_ref, poolf_ref, convw_ref, convb_ref,
                dtb_ref, alog_ref, dskip_ref, ssdn_ref, expand_ref, pools_ref, o_ref,
                wz_ref, wxbc_ref, wu_ref, wo_ref,
                xext, uext, state, z_s, xs_s, bc_s, ee_s, y_s, *, tiles_per_seq):
    i = pl.program_id(0)
    z_blocks = SSD_DIM // MIX_WCOLS
    xbc_blocks = CONV_DIM // MIX_WCOLS

    for j in range(MIX_NW):
        @pl.when(i == j)
        def _(j=j):
            blk = win_ref[...].T
            if j < z_blocks:
                wz_ref[:, j * MIX_WCOLS:(j + 1) * MIX_WCOLS] = blk.astype(BF16)
            elif j < z_blocks + xbc_blocks:
                k = j - z_blocks
                wxbc_ref[:, k * MIX_WCOLS:(k + 1) * MIX_WCOLS] = blk.astype(BF16)
            else:
                lane = lax.broadcasted_iota(jnp.int32, (D_MODEL, LANES), 1)
                rep = jnp.where(lane < SSD_HEADS, blk[:, :LANES], 0.0)
                width = SSD_HEADS
                while width < LANES:
                    rep = rep + pltpu.roll(rep, width, axis=1)
                    width *= 2
                wu_ref[:, POOL_DIM:] = rep.astype(BF16)
            if j < len(POOL_WINDOWS):
                gcols = slice(j * POOL_GROUP_DIM, (j + 1) * POOL_GROUP_DIM)
                folded = _dot(wuf_ref[...].T.astype(BF16), poolf_ref[0].astype(BF16))
                wu_ref[:, gcols] = (folded * pools_ref[:, gcols]).astype(BF16)
            if j < (SSD_DIM + POOL_DIM) // MIX_WCOLS:
                wo_ref[j * MIX_WCOLS:(j + 1) * MIX_WCOLS, :] = wof_ref[...].astype(BF16)

    @pl.when(i >= MIX_NW)
    def _():
        _mix_tile(x_ref, g_ref, wz_ref, wxbc_ref, wu_ref, convw_ref, convb_ref,
                  dtb_ref, alog_ref, dskip_ref, ssdn_ref, expand_ref,
                  wo_ref, o_ref, xext, uext, state, z_s, xs_s, bc_s, ee_s, y_s,
                  (i - MIX_NW) % tiles_per_seq)


def _mix_tile(x_ref, g_ref, wz_ref, wxbc_ref, wu_ref, convw_ref, convb_ref,
              dtb_ref, alog_ref, dskip_ref, ssdn_ref, expand_ref,
              wo_ref, o_ref, xext, uext, state, z_s, xs_s, bc_s, ee_s, y_s, sj):
    ts = MIX_TS

    @pl.when(sj == 0)
    def _():
        xext[0:CONV_HIST, :] = jnp.zeros((CONV_HIST, CONV_DIM), F32)
        uext[0:POOL_HIST, :] = jnp.zeros((POOL_HIST, POOL_DIM), F32)
        state[...] = jnp.zeros_like(state)

    x = x_ref[...]
    h = _rms(x, g_ref[...]).astype(BF16)
    xext[CONV_HIST:CONV_HIST + ts, :] = _dot(h, wxbc_ref[...])
    u_dt = _dot(h, wu_ref[...])
    uext[POOL_HIST:POOL_HIST + ts, :] = u_dt[:, :POOL_DIM]

    dt_raw = u_dt[:, POOL_DIM:] + dtb_ref[...]
    dt = jnp.maximum(dt_raw, 0.0) + jnp.log1p(jnp.exp(-jnp.abs(dt_raw)))
    cs = dt * (-jnp.exp(alog_ref[...]))
    row_in_chunk = lax.broadcasted_iota(jnp.int32, (ts, LANES), 0) % CHUNK
    shift = 1
    while shift < CHUNK:
        cs = cs + jnp.where(row_in_chunk >= shift, pltpu.roll(cs, shift, axis=0), 0.0)
        shift *= 2

    slot = lax.broadcasted_iota(jnp.int32, (ts, LANES), 1) // HEAD_SLOT
    terms = _split3(dt) + _split3(cs)
    packed = jnp.zeros((ts, LANES), F32)
    for k, term in enumerate(terms):
        packed = jnp.where(slot == k, term, packed)
    ee_s[...] = _dot(packed.astype(BF16), expand_ref[...])

    ext = xext[...]
    ext1 = pltpu.roll(ext, 1, axis=0)
    pair = convw_ref[1:2, :] * ext + convw_ref[0:1, :] * ext1
    conv = (convb_ref[...] + convw_ref[3:4, :] * ext + convw_ref[2:3, :] * ext1
            + pltpu.roll(pair, 2, axis=0))
    conv = _silu(conv[CONV_HIST:, :])
    xs_s[...] = conv[:, :SSD_DIM]
    bc_s[...] = conv[:, SSD_DIM:]
    xext[0:CONV_HIST, :] = xext[ts:ts + CONV_HIST, :]

    t_glob = sj * ts + lax.broadcasted_iota(jnp.int32, (ts, 1), 0)
    pooled = []

    def pool_group(gi):
        win = POOL_WINDOWS[gi]
        pcols = slice(gi * POOL_GROUP_DIM, (gi + 1) * POOL_GROUP_DIM)
        acc = uext[:, pcols]
        u0 = acc[POOL_HIST:, :]
        span = 1
        while span < win:
            acc = acc + pltpu.roll(acc, span, axis=0)
            span *= 2
        acc = acc[POOL_HIST:, :]
        inv_cnt = 1.0 / jnp.minimum(t_glob + 1, win).astype(F32)
        pooled.append((acc * inv_cnt - u0).astype(BF16))

    sub = lax.broadcasted_iota(jnp.int32, (CHUNK, LANES), 0)
    lane = lax.broadcasted_iota(jnp.int32, (CHUNK, LANES), 1)
    causal = sub >= lane % CHUNK
    left = lane < CHUNK
    sub8 = lax.broadcasted_iota(jnp.int32, (SUBLANES, LANES), 0)
    lane8 = lax.broadcasted_iota(jnp.int32, (SUBLANES, LANES), 1) % CHUNK
    lane_group = lane8 // SUBLANES
    on_diag = sub8 == lane8 % SUBLANES

    def diag_row(col):
        merged = col[0:SUBLANES, :]
        for i in range(1, CHUNK // SUBLANES):
            merged = jnp.where(lane_group == i, col[i * SUBLANES:(i + 1) * SUBLANES, :], merged)
        return jnp.sum(jnp.where(on_diag, merged, 0.0), axis=0, keepdims=True)
    for c in range(ts // CHUNK):
        rows = slice(c * CHUNK, (c + 1) * CHUNK)
        b_all = bc_s[rows, 0:BC_DIM]
        cb_lhs = bc_s[rows, BC_DIM:2 * BC_DIM].astype(BF16)
        zero_b = jnp.zeros((2 * CHUNK, SSD_STATE), BF16)
        cb_rhs = []
        for g in range(SSD_NGROUPS):
            b_b = b_all[:, g * SSD_STATE:(g + 1) * SSD_STATE].astype(BF16)
            b2 = jnp.concatenate([b_b, b_b], axis=0)
            cb_rhs.append(jnp.concatenate(
                [b2 if k == g else zero_b for k in range(SSD_NGROUPS)], axis=-1))
        cb_all = lax.dot_general(cb_lhs, jnp.concatenate(cb_rhs, axis=0),
                                 (((1,), (1,)), ((), ())), preferred_element_type=F32)
        for g in range(SSD_NGROUPS):
            c_b = cb_lhs[:, g * SSD_STATE:(g + 1) * SSD_STATE]
            bt_b = b_all[:, g * SSD_STATE:(g + 1) * SSD_STATE].T.astype(BF16)
            cb2 = cb_all[:, g * 2 * CHUNK:(g + 1) * 2 * CHUNK]
            for quad in range(GROUP_DIM // (2 * LANES)):
                qc = slice(2 * quad * LANES, 2 * (quad + 1) * LANES)
                oc = slice(g * GROUP_DIM + qc.start, g * GROUP_DIM + qc.stop)
                xq = xs_s[rows, oc]
                cs_q = ee_s[rows, SSD_DIM + oc.start:SSD_DIM + oc.stop]
                cs_last = cs_q[CHUNK - 1:CHUNK, :]
                xdt = xq * ee_s[rows, oc]
                xdt_b = xdt.astype(BF16)
                x_decayed = (xdt * jnp.exp(cs_last - cs_q)).astype(BF16)
                s_prev = state[g, :, qc]
                y_off = _dot(c_b, s_prev.astype(BF16)) * jnp.exp(cs_q)
                state[g, :, qc] = jnp.exp(cs_last) * s_prev + _dot(bt_b, x_decayed)
                scores, blocks = [], []
                for half in range(2):
                    pc = slice(half * LANES, (half + 1) * LANES)
                    col = cs_q[:, pc]
                    rowv = diag_row(col)
                    decay = jnp.exp(jnp.where(causal, col - rowv, -jnp.inf))
                    scores.append((cb2 * decay).astype(BF16))
                    xp = xdt_b[:, pc]
                    zero = jnp.zeros_like(xp)
                    for blk in (jnp.where(left, xp, zero), jnp.where(left, zero, xp)):
                        blocks.append(jnp.concatenate([blk, zero] if half == 0 else [zero, blk], axis=-1))
                y_s[rows, oc] = (_dot(jnp.concatenate(scores, axis=-1), jnp.concatenate(blocks, axis=0))
                                 + y_off + dskip_ref[:, oc] * xq)
        if c % 2 == 1:
            pool_group(c // 2)

    z_s[...] = _dot(h, wz_ref[...])
    yg = y_s[...] * _silu(z_s[...])
    normed = []
    for g in range(SSD_NGROUPS):
        seg = yg[:, g * GROUP_DIM:(g + 1) * GROUP_DIM]
        normed.append(seg * lax.rsqrt(jnp.mean(seg * seg, axis=-1, keepdims=True) + NORM_EPS))
    y_ssd = (jnp.concatenate(normed, axis=-1) * ssdn_ref[...]).astype(BF16)

    y_pool = jnp.concatenate(pooled, axis=-1)
    uext[0:POOL_HIST, :] = uext[ts:ts + POOL_HIST, :]

    o_ref[...] = x + _dot(jnp.concatenate([y_ssd, y_pool], axis=-1), wo_ref[...])


def _expand_matrix():
    e = np.zeros((LANES, 2 * SSD_DIM), np.float32)
    for k in range(6):
        for hd in range(SSD_HEADS):
            base = (k // 3) * SSD_DIM + hd * SSD_HEAD_DIM
            e[k * HEAD_SLOT + hd, base:base + SSD_HEAD_DIM] = 1.0
    return e


def _mix(x2d, seq_len, gain, w_in, u_row0, w_out, pool_w, conv_w, conv_b, dt_bias, a_log, d_skip_e,
         ssd_norm, pool_scale):
    m, d = x2d.shape
    ts = MIX_TS
    tiles_per_seq = seq_len // ts
    expand = jnp.asarray(_expand_matrix(), BF16)
    small = (conv_w, conv_b, dt_bias, a_log, d_skip_e, ssd_norm, expand, pool_scale)
    tile = pl.BlockSpec((ts, d), lambda i: (jnp.maximum(i - MIX_NW, 0), 0))
    n_u = POOL_DIM // MIX_WCOLS
    n_o = (SSD_DIM + POOL_DIM) // MIX_WCOLS
    n_p = len(POOL_WINDOWS)
    in_specs = [
        tile,
        _resident(gain.shape),
        pl.BlockSpec((MIX_WCOLS, d), lambda i: (jnp.minimum(i, MIX_NW - 1), 0)),
        pl.BlockSpec((pl.Element(MIX_WCOLS), pl.Element(d)),
                     lambda i: (pl.multiple_of(u_row0 + jnp.minimum(i, n_u - 1) * MIX_WCOLS, SUBLANES), 0)),
        pl.BlockSpec((MIX_WCOLS, d), lambda i: (jnp.minimum(i, n_o - 1), 0)),
        pl.BlockSpec((1, POOL_GROUP_DIM, POOL_GROUP_DIM), lambda i: (jnp.minimum(i, n_p - 1), 0, 0)),
    ]
    in_specs += [_resident(w.shape) for w in small]
    return pl.pallas_call(
        functools.partial(_mix_kernel, tiles_per_seq=tiles_per_seq),
        out_shape=jax.ShapeDtypeStruct((m, d), F32),
        grid=(MIX_NW + m // ts,),
        in_specs=in_specs,
        out_specs=tile,
        scratch_shapes=[
            pltpu.VMEM((d, SSD_DIM), BF16),
            pltpu.VMEM((d, CONV_DIM), BF16),
            pltpu.VMEM((d, POOL_DIM + LANES), BF16),
            pltpu.VMEM((SSD_DIM + POOL_DIM, d), BF16),
            pltpu.VMEM((CONV_HIST + ts, CONV_DIM), F32),
            pltpu.VMEM((POOL_HIST + ts, POOL_DIM), F32),
            pltpu.VMEM((SSD_NGROUPS, SSD_STATE, GROUP_DIM), F32),
            pltpu.VMEM((ts, SSD_DIM), F32),
            pltpu.VMEM((ts, SSD_DIM), F32),
            pltpu.VMEM((ts, 2 * BC_DIM), F32),
            pltpu.VMEM((ts, 2 * SSD_DIM), F32),
            pltpu.VMEM((ts, SSD_DIM), F32),
        ],
        compiler_params=pltpu.CompilerParams(
            dimension_semantics=("arbitrary",), vmem_limit_bytes=VMEM_LIMIT),
        name="mix",
    )(x2d, gain, w_in, w_in, w_out, pool_w, *small)


def kernel(x, ffn1_norm, ffn1_w_gate, ffn1_w_up, ffn1_w_down, mix_norm, w_in, conv_w, conv_b,
           dt_bias, a_log, d_skip, ssd_norm, pool_w, pool_scale, w_out, ffn2_norm,
           ffn2_w_gate, ffn2_w_up, ffn2_w_down, final_norm):
    b, s, d = x.shape
    assert ffn1_norm.shape[0] == 1, "one macaron block"
    o3 = SSD_DIM + CONV_DIM + SSD_HEADS
    assert MIX_NW * MIX_WCOLS >= o3 and (MIX_NW - 1) * MIX_WCOLS == o3 - SSD_HEADS
    rep = LANES // SSD_HEADS
    row = lambda v: v.reshape(1, -1)
    ones = jnp.ones((1, d), F32)
    x2d = _ffn(x.reshape(b * s, d), row(ffn1_norm[0]), ffn1_w_gate[0], ffn1_w_up[0],
               ffn1_w_down[0], ones, final_norm=False, name="ffn1")
    x2d = _mix(
        x2d, s, row(mix_norm[0]), w_in[0].T, o3, w_out[0], pool_w[0],
        conv_w[0], row(conv_b[0]),
        jnp.tile(row(dt_bias[0]), (1, rep)), jnp.tile(row(a_log[0]), (1, rep)),
        row(jnp.repeat(d_skip[0], SSD_HEAD_DIM)), row(ssd_norm[0]), row(pool_scale[0]))
    out = _ffn(x2d, row(ffn2_norm[0]), ffn2_w_gate[0], ffn2_w_up[0], ffn2_w_down[0],
               row(final_norm), final_norm=True, name="ffn2")
    return out.reshape(b, s, d)
```

```python
import functools

import numpy as np
import jax
import jax.numpy as jnp
from jax import lax
from jax.experimental import pallas as pl
from jax.experimental.pallas import tpu as pltpu

D_MODEL = 1024
CHUNK = 64
SSD_DIM = 1024
SSD_HEAD_DIM = 64
SSD_HEADS = 16
SSD_NGROUPS = 2
SSD_STATE = 128
SSD_CONV = 4
POOL_DIM = 1024
POOL_WINDOWS = (2, 4, 8, 16)
POOL_GROUP_DIM = 256
CONV_DIM = SSD_DIM + 2 * SSD_NGROUPS * SSD_STATE
D_FF = 2816
NORM_EPS = 1e-6

GROUP_DIM = SSD_DIM // SSD_NGROUPS
BC_DIM = SSD_NGROUPS * SSD_STATE
LANES = 128
SUBLANES = 8
HEAD_SLOT = 16
CONV_HIST = SUBLANES
POOL_HIST = 16

FFN_TM = 1024
FFN_TF = 256
FFN_NW = D_FF // FFN_TF
MIX_TS = 512
MIX_WCOLS = 256
MIX_NW = 11
VMEM_LIMIT = 56 * 1024 * 1024

F32 = jnp.float32
BF16 = jnp.bfloat16


def _silu(v):
    half = 0.5 * v
    return half + half * jnp.tanh(half)


def _rms(v, gain):
    return v * lax.rsqrt(jnp.mean(v * v, axis=-1, keepdims=True) + NORM_EPS) * gain


def _dot(a, b):
    return jnp.dot(a, b, preferred_element_type=F32)


def _ffn_kernel(x_ref, g_ref, wg_ref, wu_ref, wd_ref, fg_ref, o_ref,
                wg_s, wu_s, wd_s, act_ref, *, final_norm):
    i = pl.program_id(0)

    @pl.when(i < FFN_NW)
    def _():
        wg_s[i] = wg_ref[...].astype(BF16)
        wu_s[i] = wu_ref[...].astype(BF16)
        wd_s[pl.ds(pl.multiple_of(i * FFN_TF, FFN_TF), FFN_TF), :] = wd_ref[...].astype(BF16)

    @pl.when(i >= FFN_NW)
    def _():
        x = x_ref[...]
        h = _rms(x, g_ref[...]).astype(BF16)
        for j in range(FFN_NW):
            gate = _dot(h, wg_s[j])
            up = _dot(h, wu_s[j])
            act_ref[:, j * FFN_TF:(j + 1) * FFN_TF] = (_silu(gate) * up).astype(BF16)
        y = x + 0.5 * _dot(act_ref[...], wd_s[...])
        if final_norm:
            y = _rms(y, fg_ref[...])
        o_ref[...] = y


def _resident(shape):
    return pl.BlockSpec(shape, lambda *_: (0,) * len(shape), pipeline_mode=pl.Buffered(1))


def _ffn(x2d, gain, w_gate, w_up, w_down, final_gain, *, final_norm, name):
    m = x2d.shape[0]
    wchunk = lambda i: jnp.minimum(i, FFN_NW - 1)
    tile = pl.BlockSpec((FFN_TM, D_MODEL), lambda i: (jnp.maximum(i - FFN_NW, 0), 0))
    return pl.pallas_call(
        functools.partial(_ffn_kernel, final_norm=final_norm),
        out_shape=jax.ShapeDtypeStruct((m, D_MODEL), F32),
        grid=(FFN_NW + m // FFN_TM,),
        in_specs=[
            tile,
            _resident((1, D_MODEL)),
            pl.BlockSpec((D_MODEL, FFN_TF), lambda i: (0, wchunk(i))),
            pl.BlockSpec((D_MODEL, FFN_TF), lambda i: (0, wchunk(i))),
            pl.BlockSpec((FFN_TF, D_MODEL), lambda i: (wchunk(i), 0)),
            _resident((1, D_MODEL)),
        ],
        out_specs=tile,
        scratch_shapes=[
            pltpu.VMEM((FFN_NW, D_MODEL, FFN_TF), BF16),
            pltpu.VMEM((FFN_NW, D_MODEL, FFN_TF), BF16),
            pltpu.VMEM((D_FF, D_MODEL), BF16),
            pltpu.VMEM((FFN_TM, D_FF), BF16),
        ],
        compiler_params=pltpu.CompilerParams(
            dimension_semantics=("arbitrary",), vmem_limit_bytes=VMEM_LIMIT),
        name=name,
    )(x2d, gain, w_gate, w_up, w_down, final_gain)


def _split3(v):
    hi = v.astype(BF16).astype(F32)
    r = v - hi
    mid = r.astype(BF16).astype(F32)
    return hi, mid, r - mid


def _mix_kernel(x_ref, g_ref, win_ref, wuf_ref, wof_ref, poolf_ref, convw_ref, convb_ref,
                dtb_ref, alog_ref, dskip_ref, ssdn_ref, expand_ref, pools_ref, o_ref,
                wz_ref, wxbc_ref, wu_ref, wo_ref,
                xext, uext, state, z_s, xs_s, bc_s, ee_s, y_s, *, tiles_per_seq):
    i = pl.program_id(0)
    z_blocks = SSD_DIM // MIX_WCOLS
    xbc_blocks = CONV_DIM // MIX_WCOLS

    for j in range(MIX_NW):
        @pl.when(i == j)
        def _(j=j):
            blk = win_ref[...].T
            if j < z_blocks:
                wz_ref[:, j * MIX_WCOLS:(j + 1) * MIX_WCOLS] = blk.astype(BF16)
            elif j < z_blocks + xbc_blocks:
                k = j - z_blocks
                wxbc_ref[:, k * MIX_WCOLS:(k + 1) * MIX_WCOLS] = blk.astype(BF16)
            else:
                lane = lax.broadcasted_iota(jnp.int32, (D_MODEL, LANES), 1)
                rep = jnp.where(lane < SSD_HEADS, blk[:, :LANES], 0.0)
                width = SSD_HEADS
                while width < LANES:
                    rep = rep + pltpu.roll(rep, width, axis=1)
                    width *= 2
                wu_ref[:, POOL_DIM:] = rep.astype(BF16)
            if j < len(POOL_WINDOWS):
                gcols = slice(j * POOL_GROUP_DIM, (j + 1) * POOL_GROUP_DIM)
                folded = _dot(wuf_ref[...].T.astype(BF16), poolf_ref[0].astype(BF16))
                wu_ref[:, gcols] = (folded * pools_ref[:, gcols]).astype(BF16)
            if j < (SSD_DIM + POOL_DIM) // MIX_WCOLS:
                wo_ref[j * MIX_WCOLS:(j + 1) * MIX_WCOLS, :] = wof_ref[...].astype(BF16)

    @pl.when(i >= MIX_NW)
    def _():
        _mix_tile(x_ref, g_ref, wz_ref, wxbc_ref, wu_ref, convw_ref, convb_ref,
                  dtb_ref, alog_ref, dskip_ref, ssdn_ref, expand_ref,
                  wo_ref, o_ref, xext, uext, state, z_s, xs_s, bc_s, ee_s, y_s,
                  (i - MIX_NW) % tiles_per_seq)


def _mix_tile(x_ref, g_ref, wz_ref, wxbc_ref, wu_ref, convw_ref, convb_ref,
              dtb_ref, alog_ref, dskip_ref, ssdn_ref, expand_ref,
              wo_ref, o_ref, xext, uext, state, z_s, xs_s, bc_s, ee_s, y_s, sj):
    ts = MIX_TS

    @pl.when(sj == 0)
    def _():
        xext[0:CONV_HIST, :] = jnp.zeros((CONV_HIST, CONV_DIM), F32)
        uext[0:POOL_HIST, :] = jnp.zeros((POOL_HIST, POOL_DIM), F32)
        state[...] = jnp.zeros_like(state)

    x = x_ref[...]
    h = _rms(x, g_ref[...]).astype(BF16)
    xext[CONV_HIST:CONV_HIST + ts, :] = _dot(h, wxbc_ref[...])
    u_dt = _dot(h, wu_ref[...])
    uext[POOL_HIST:POOL_HIST + ts, :] = u_dt[:, :POOL_DIM]

    dt_raw = u_dt[:, POOL_DIM:] + dtb_ref[...]
    dt = jnp.maximum(dt_raw, 0.0) + jnp.log1p(jnp.exp(-jnp.abs(dt_raw)))
    cs = dt * (-jnp.exp(alog_ref[...]))
    row_in_chunk = lax.broadcasted_iota(jnp.int32, (ts, LANES), 0) % CHUNK
    shift = 1
    while shift < CHUNK:
        cs = cs + jnp.where(row_in_chunk >= shift, pltpu.roll(cs, shift, axis=0), 0.0)
        shift *= 2

    slot = lax.broadcasted_iota(jnp.int32, (ts, LANES), 1) // HEAD_SLOT
    terms = _split3(dt) + _split3(cs)
    packed = jnp.zeros((ts, LANES), F32)
    for k, term in enumerate(terms):
        packed = jnp.where(slot == k, term, packed)
    ee_s[...] = _dot(packed.astype(BF16), expand_ref[...])

    ext = xext[...]
    ext1 = pltpu.roll(ext, 1, axis=0)
    pair = convw_ref[1:2, :] * ext + convw_ref[0:1, :] * ext1
    conv = (convb_ref[...] + convw_ref[3:4, :] * ext + convw_ref[2:3, :] * ext1
            + pltpu.roll(pair, 2, axis=0))
    conv = _silu(conv[CONV_HIST:, :])
    xs_s[...] = conv[:, :SSD_DIM]
    bc_s[...] = conv[:, SSD_DIM:]
    xext[0:CONV_HIST, :] = xext[ts:ts + CONV_HIST, :]

    t_glob = sj * ts + lax.broadcasted_iota(jnp.int32, (ts, 1), 0)
    pooled = []

    def pool_group(gi):
        win = POOL_WINDOWS[gi]
        pcols = slice(gi * POOL_GROUP_DIM, (gi + 1) * POOL_GROUP_DIM)
        acc = uext[:, pcols]
        u0 = acc[POOL_HIST:, :]
        span = 1
        while span < win:
            acc = acc + pltpu.roll(acc, span, axis=0)
            span *= 2
        acc = acc[POOL_HIST:, :]
        inv_cnt = 1.0 / jnp.minimum(t_glob + 1, win).astype(F32)
        pooled.append((acc * inv_cnt - u0).astype(BF16))

    sub = lax.broadcasted_iota(jnp.int32, (CHUNK, LANES), 0)
    lane = lax.broadcasted_iota(jnp.int32, (CHUNK, LANES), 1)
    causal = sub >= lane % CHUNK
    left = lane < CHUNK
    sub8 = lax.broadcasted_iota(jnp.int32, (SUBLANES, LANES), 0)
    lane8 = lax.broadcasted_iota(jnp.int32, (SUBLANES, LANES), 1) % CHUNK
    lane_group = lane8 // SUBLANES
    on_diag = sub8 == lane8 % SUBLANES

    def diag_row(col):
        merged = col[0:SUBLANES, :]
        for i in range(1, CHUNK // SUBLANES):
            merged = jnp.where(lane_group == i, col[i * SUBLANES:(i + 1) * SUBLANES, :], merged)
        return jnp.sum(jnp.where(on_diag, merged, 0.0), axis=0, keepdims=True)
    def chunk_scores(c):
        rows = slice(c * CHUNK, (c + 1) * CHUNK)
        b_all = bc_s[rows, 0:BC_DIM]
        cb_lhs = bc_s[rows, BC_DIM:2 * BC_DIM].astype(BF16)
        zero_b = jnp.zeros((2 * CHUNK, SSD_STATE), BF16)
        cb_rhs = []
        for g in range(SSD_NGROUPS):
            b_b = b_all[:, g * SSD_STATE:(g + 1) * SSD_STATE].astype(BF16)
            b2 = jnp.concatenate([b_b, b_b], axis=0)
            cb_rhs.append(jnp.concatenate(
                [b2 if k == g else zero_b for k in range(SSD_NGROUPS)], axis=-1))
        cb_all = lax.dot_general(cb_lhs, jnp.concatenate(cb_rhs, axis=0),
                                 (((1,), (1,)), ((), ())), preferred_element_type=F32)
        return b_all, cb_lhs, cb_all

    upcoming = chunk_scores(0)
    for c in range(ts // CHUNK):
        rows = slice(c * CHUNK, (c + 1) * CHUNK)
        b_all, cb_lhs, cb_all = upcoming
        if c + 1 < ts // CHUNK:
            upcoming = chunk_scores(c + 1)
        for g in range(SSD_NGROUPS):
            c_b = cb_lhs[:, g * SSD_STATE:(g + 1) * SSD_STATE]
            bt_b = b_all[:, g * SSD_STATE:(g + 1) * SSD_STATE].T.astype(BF16)
            cb2 = cb_all[:, g * 2 * CHUNK:(g + 1) * 2 * CHUNK]
            for quad in range(GROUP_DIM // (2 * LANES)):
                qc = slice(2 * quad * LANES, 2 * (quad + 1) * LANES)
                oc = slice(g * GROUP_DIM + qc.start, g * GROUP_DIM + qc.stop)
                xq = xs_s[rows, oc]
                cs_q = ee_s[rows, SSD_DIM + oc.start:SSD_DIM + oc.stop]
                cs_last = cs_q[CHUNK - 1:CHUNK, :]
                xdt = xq * ee_s[rows, oc]
                xdt_b = xdt.astype(BF16)
                x_decayed = (xdt * jnp.exp(cs_last - cs_q)).astype(BF16)
                s_prev = state[g, :, qc]
                y_off = _dot(c_b, s_prev.astype(BF16)) * jnp.exp(cs_q)
                state[g, :, qc] = jnp.exp(cs_last) * s_prev + _dot(bt_b, x_decayed)
                scores, blocks = [], []
                for half in range(2):
                    pc = slice(half * LANES, (half + 1) * LANES)
                    col = cs_q[:, pc]
                    rowv = diag_row(col)
                    decay = jnp.exp(jnp.where(causal, col - rowv, -jnp.inf))
                    scores.append((cb2 * decay).astype(BF16))
                    xp = xdt_b[:, pc]
                    zero = jnp.zeros_like(xp)
                    for blk in (jnp.where(left, xp, zero), jnp.where(left, zero, xp)):
                        blocks.append(jnp.concatenate([blk, zero] if half == 0 else [zero, blk], axis=-1))
                y_s[rows, oc] = (_dot(jnp.concatenate(scores, axis=-1), jnp.concatenate(blocks, axis=0))
                                 + y_off + dskip_ref[:, oc] * xq)
        if c % 2 == 1:
            pool_group(c // 2)

    z_s[...] = _dot(h, wz_ref[...])
    yg = y_s[...] * _silu(z_s[...])
    normed = []
    for g in range(SSD_NGROUPS):
        seg = yg[:, g * GROUP_DIM:(g + 1) * GROUP_DIM]
        normed.append(seg * lax.rsqrt(jnp.mean(seg * seg, axis=-1, keepdims=True) + NORM_EPS))
    y_ssd = (jnp.concatenate(normed, axis=-1) * ssdn_ref[...]).astype(BF16)

    y_pool = jnp.concatenate(pooled, axis=-1)
    uext[0:POOL_HIST, :] = uext[ts:ts + POOL_HIST, :]

    o_ref[...] = x + _dot(jnp.concatenate([y_ssd, y_pool], axis=-1), wo_ref[...])


def _expand_matrix():
    e = np.zeros((LANES, 2 * SSD_DIM), np.float32)
    for k in range(6):
        for hd in range(SSD_HEADS):
            base = (k // 3) * SSD_DIM + hd * SSD_HEAD_DIM
            e[k * HEAD_SLOT + hd, base:base + SSD_HEAD_DIM] = 1.0
    return e


def _mix(x2d, seq_len, gain, w_in, u_row0, w_out, pool_w, conv_w, conv_b, dt_bias, a_log, d_skip_e,
         ssd_norm, pool_scale):
    m, d = x2d.shape
    ts = MIX_TS
    tiles_per_seq = seq_len // ts
    expand = jnp.asarray(_expand_matrix(), BF16)
    small = (conv_w, conv_b, dt_bias, a_log, d_skip_e, ssd_norm, expand, pool_scale)
    tile = pl.BlockSpec((ts, d), lambda i: (jnp.maximum(i - MIX_NW, 0), 0))
    n_u = POOL_DIM // MIX_WCOLS
    n_o = (SSD_DIM + POOL_DIM) // MIX_WCOLS
    n_p = len(POOL_WINDOWS)
    in_specs = [
        tile,
        _resident(gain.shape),
        pl.BlockSpec((MIX_WCOLS, d), lambda i: (jnp.minimum(i, MIX_NW - 1), 0)),
        pl.BlockSpec((pl.Element(MIX_WCOLS), pl.Element(d)),
                     lambda i: (pl.multiple_of(u_row0 + jnp.minimum(i, n_u - 1) * MIX_WCOLS, SUBLANES), 0)),
        pl.BlockSpec((MIX_WCOLS, d), lambda i: (jnp.minimum(i, n_o - 1), 0)),
        pl.BlockSpec((1, POOL_GROUP_DIM, POOL_GROUP_DIM), lambda i: (jnp.minimum(i, n_p - 1), 0, 0)),
    ]
    in_specs += [_resident(w.shape) for w in small]
    return pl.pallas_call(
        functools.partial(_mix_kernel, tiles_per_seq=tiles_per_seq),
        out_shape=jax.ShapeDtypeStruct((m, d), F32),
        grid=(MIX_NW + m // ts,),
        in_specs=in_specs,
        out_specs=tile,
        scratch_shapes=[
            pltpu.VMEM((d, SSD_DIM), BF16),
            pltpu.VMEM((d, CONV_DIM), BF16),
            pltpu.VMEM((d, POOL_DIM + LANES), BF16),
            pltpu.VMEM((SSD_DIM + POOL_DIM, d), BF16),
            pltpu.VMEM((CONV_HIST + ts, CONV_DIM), F32),
            pltpu.VMEM((POOL_HIST + ts, POOL_DIM), F32),
            pltpu.VMEM((SSD_NGROUPS, SSD_STATE, GROUP_DIM), F32),
            pltpu.VMEM((ts, SSD_DIM), F32),
            pltpu.VMEM((ts, SSD_DIM), F32),
            pltpu.VMEM((ts, 2 * BC_DIM), F32),
            pltpu.VMEM((ts, 2 * SSD_DIM), F32),
            pltpu.VMEM((ts, SSD_DIM), F32),
        ],
        compiler_params=pltpu.CompilerParams(
            dimension_semantics=("arbitrary",), vmem_limit_bytes=VMEM_LIMIT),
        name="mix",
    )(x2d, gain, w_in, w_in, w_out, pool_w, *small)


def kernel(x, ffn1_norm, ffn1_w_gate, ffn1_w_up, ffn1_w_down, mix_norm, w_in, conv_w, conv_b,
           dt_bias, a_log, d_skip, ssd_norm, pool_w, pool_scale, w_out, ffn2_norm,
           ffn2_w_gate, ffn2_w_up, ffn2_w_down, final_norm):
    b, s, d = x.shape
    assert ffn1_norm.shape[0] == 1, "one macaron block"
    o3 = SSD_DIM + CONV_DIM + SSD_HEADS
    assert MIX_NW * MIX_WCOLS >= o3 and (MIX_NW - 1) * MIX_WCOLS == o3 - SSD_HEADS
    rep = LANES // SSD_HEADS
    row = lambda v: v.reshape(1, -1)
    ones = jnp.ones((1, d), F32)
    x2d = _ffn(x.reshape(b * s, d), row(ffn1_norm[0]), ffn1_w_gate[0], ffn1_w_up[0],
               ffn1_w_down[0], ones, final_norm=False, name="ffn1")
    x2d = _mix(
        x2d, s, row(mix_norm[0]), w_in[0].T, o3, w_out[0], pool_w[0],
        conv_w[0], row(conv_b[0]),
        jnp.tile(row(dt_bias[0]), (1, rep)), jnp.tile(row(a_log[0]), (1, rep)),
        row(jnp.repeat(d_skip[0], SSD_HEAD_DIM)), row(ssd_norm[0]), row(pool_scale[0]))
    out = _ffn(x2d, row(ffn2_norm[0]), ffn2_w_gate[0], ffn2_w_up[0], ffn2_w_down[0],
               row(final_norm), final_norm=True, name="ffn2")
    return out.reshape(b, s, d)
```
